```python
import functools
import jax, jax.numpy as jnp
from jax import lax
import numpy as np

D_MODEL = 4096
BATCH = 4
SEQ = 2048
DEPTH = 2
DEC_BATCH = 8
DEC_SEQ = 8
PAST_LEN = 16384
PAGE_SIZE = 128

RET_QK_DIM = 256
RET_V_DIM = 256
RET_HEADS = D_MODEL // 512
SB_HEAD_DIM = 128
SB_HEADS = D_MODEL // 256
SB_BIAS_INIT = -6.0
D_RET_QK = RET_HEADS * RET_QK_DIM
D_RET_V = RET_HEADS * RET_V_DIM
D_SB = SB_HEADS * SB_HEAD_DIM
RET_CHUNK = 128
SB_BLOCK = 128
ROPE_BASE = 10000.0
D_FF = 7 * D_MODEL // 2
N_EXPERTS = 8
TOP_K = 2
N_DENSE = (DEPTH + 1) // 2
N_MOE = DEPTH // 2
NORM_EPS = 1e-6
IN_SIZES = (D_RET_QK, D_RET_QK, D_RET_V, D_RET_V, D_SB, D_SB, D_SB, D_MODEL, D_MODEL)
IN_COLS = 2 * D_RET_QK + 2 * D_RET_V + 3 * D_SB + 2 * D_MODEL

kernel_name = 'hybrid_retention_stickbreak_decoder_step'


def rms_norm(x, g):
    xf = x.astype(jnp.float32)
    y = xf * lax.rsqrt(jnp.mean(xf * xf, axis=-1, keepdims=True) + NORM_EPS)
    return (y * g.astype(jnp.float32)).astype(x.dtype)


def head_rms(x):
    return x * lax.rsqrt(jnp.mean(x * x, axis=-1, keepdims=True) + NORM_EPS)


def rotary(x, pos):
    half = x.shape[-1] // 2
    inv_freq = jnp.power(ROPE_BASE, -jnp.linspace(0.0, 1.0, half, dtype=jnp.float32))
    ang = pos.astype(jnp.float32)[:, None] * inv_freq[None, :]
    cos = jnp.cos(ang)[None, :, None, :]
    sin = jnp.sin(ang)[None, :, None, :]
    x1, x2 = x[..., :half], x[..., half:]
    return jnp.concatenate([x1 * cos - x2 * sin, x2 * cos + x1 * sin], axis=-1)


def retention_log_decay():
    return jnp.log1p(-jnp.exp2(-5.0 - jnp.arange(RET_HEADS, dtype=jnp.float32)))


def retention_chunk(q, k, v, S, log_g):
    C = q.shape[1]
    idx = jnp.arange(C, dtype=jnp.float32)
    diff = idx[:, None] - idx[None, :]
    decay = jnp.where(diff >= 0, jnp.exp(jnp.maximum(diff, 0.0)[None] * log_g[:, None, None]), 0.0)
    scores = jnp.einsum('bihd,bjhd->bhij', q, k) * decay[None]
    o_intra = jnp.einsum('bhij,bjhe->bihe', scores, v)
    q_decay = jnp.exp((idx[None, :] + 1.0) * log_g[:, None])
    o_cross = jnp.einsum('bihd,bhde->bihe', q, S) * q_decay.T[None, :, :, None]
    k_decay = jnp.exp((C - 1.0 - idx)[None, :] * log_g[:, None])
    S_new = jnp.exp(C * log_g)[None, :, None, None] * S + jnp.einsum('bjhd,hj,bjhe->bhde', k, k_decay, v)
    return o_intra + o_cross, S_new


def retention_prompt(q, k, v, log_g):
    B, T, H, dk = q.shape
    dv = v.shape[-1]
    n = T // RET_CHUNK

    def to_chunks(a):
        return a.reshape(B, n, RET_CHUNK, H, a.shape[-1]).transpose(1, 0, 2, 3, 4)

    def step(S, qkv):
        o, S_new = retention_chunk(qkv[0], qkv[1], qkv[2], S, log_g)
        return S_new, o

    S0 = jnp.zeros((B, H, dk, dv), jnp.float32)
    S, o = lax.scan(step, S0, (to_chunks(q), to_chunks(k), to_chunks(v)))
    return o.transpose(1, 0, 2, 3, 4).reshape(B, T, H, dv), S


def sb_weights(z, causal):
    axis = z.ndim - 1
    log_not = jnp.where(causal, jax.nn.log_sigmoid(-z), 0.0)
    suffix = lax.cumsum(log_not, axis=axis, reverse=True)
    excl = jnp.concatenate([suffix[..., 1:], jnp.zeros_like(suffix[..., :1])], axis=-1)
    return jnp.where(causal, jnp.exp(jax.nn.log_sigmoid(z) + excl), 0.0)


def sb_prompt(q, k, v, bias):
    B, T, H, d = q.shape
    nb = T // SB_BLOCK
    scale = d ** -0.5
    kf = k.astype(jnp.float32)
    vf = v.astype(jnp.float32)
    bf = bias.astype(jnp.float32)[None, :, None, None]
    kpos = jnp.arange(T, dtype=jnp.int32)
    qb = q.astype(jnp.float32).reshape(B, nb, SB_BLOCK, H, d).transpose(1, 0, 2, 3, 4)
    qpos = kpos.reshape(nb, SB_BLOCK)

    def block(args):
        qi, pi = args
        z = jnp.einsum('bqhd,bnhd->bhqn', qi, kf) * scale + bf
        w = sb_weights(z, kpos[None, :] < pi[:, None])
        return jnp.einsum('bhqn,bnhd->bqhd', w, vf)

    o = lax.map(block, (qb, qpos))
    return o.transpose(1, 0, 2, 3, 4).reshape(B, T, H, d)


def sb_sample(q, k_new, v_new, k_past, v_past, q_pos, bias):
    P = k_past.shape[1]
    Q = q.shape[1]
    scale = q.shape[-1] ** -0.5
    qf = q.astype(jnp.float32)
    kpos = jnp.arange(P + Q, dtype=jnp.int32)
    z = jnp.concatenate([jnp.einsum('bqhd,bnhd->bhqn', qf, k_past.astype(jnp.float32)),
                         jnp.einsum('bqhd,bnhd->bhqn', qf, k_new.astype(jnp.float32))], axis=-1) * scale
    z = z + bias.astype(jnp.float32)[None, :, None, None]
    w = sb_weights(z, kpos[None, :] < q_pos[:, None])
    return (jnp.einsum('bhqn,bnhd->bqhd', w[..., :P], v_past.astype(jnp.float32))
            + jnp.einsum('bhqn,bnhd->bqhd', w[..., P:], v_new.astype(jnp.float32)))


def token_mixer(h, pos, ret_core, sb_core, w_in_l, b_sb_l, w_ret_br_l, w_sb_br_l, w_o_l):
    B, T, _ = h.shape
    f32 = jnp.float32
    cuts, acc = [], 0
    for size in IN_SIZES[:-1]:
        acc += size
        cuts.append(acc)
    q_r, k_r, v_r, g_r, q_s, k_s, v_s, a_r, a_s = jnp.split(h @ w_in_l, cuts, axis=-1)
    q_r = rotary(q_r.reshape(B, T, RET_HEADS, RET_QK_DIM).astype(f32), pos)
    k_r = rotary(k_r.reshape(B, T, RET_HEADS, RET_QK_DIM).astype(f32), pos) * RET_QK_DIM ** -0.5
    v_r = v_r.reshape(B, T, RET_HEADS, RET_V_DIM).astype(f32)
    o_r, s_r = ret_core(q_r, k_r, v_r)
    r = (head_rms(o_r).reshape(B, T, D_RET_V) * jax.nn.silu(g_r.astype(f32))).astype(h.dtype)
    k_s = k_s.reshape(B, T, SB_HEADS, SB_HEAD_DIM)
    v_s = v_s.reshape(B, T, SB_HEADS, SB_HEAD_DIM)
    o_s = sb_core(q_s.reshape(B, T, SB_HEADS, SB_HEAD_DIM), k_s, v_s, b_sb_l)
    s = o_s.reshape(B, T, D_SB).astype(h.dtype)
    u = jax.nn.sigmoid(a_r) * (r @ w_ret_br_l) + jax.nn.sigmoid(a_s) * (s @ w_sb_br_l)
    return u @ w_o_l, s_r, k_s, v_s


def adaln(c, w, b):
    mod = jax.nn.silu(c) @ w + b
    return [m[:, None, :] for m in jnp.split(mod, 6, axis=-1)]


def swiglu(h, w_gate, w_up, w_down):
    return (jax.nn.silu(h @ w_gate) * (h @ w_up)) @ w_down


def moe_swiglu(h, w_router, b_router, w_gate, w_up, w_down):
    logits = (h @ w_router).astype(jnp.float32) + b_router.astype(jnp.float32)
    top_val, top_idx = lax.top_k(logits, TOP_K)
    top_w = jax.nn.softmax(top_val, axis=-1)
    comb = jnp.einsum('...k,...ke->...e', top_w,
                      jax.nn.one_hot(top_idx, N_EXPERTS, dtype=jnp.float32)).astype(h.dtype)
    out = jnp.zeros(h.shape[:-1] + (w_down.shape[-1],), h.dtype)
    for e in range(N_EXPERTS):
        out = out + comb[..., e:e + 1] * swiglu(h, w_gate[e], w_up[e], w_down[e])
    return out


def decoder_layer(x, c, pos, ret_core, sb_core, w_ada_l, b_ada_l, g_mix, g_ffn,
                  w_in_l, b_sb_l, w_ret_br_l, w_sb_br_l, w_o_l, ffn):
    sh_m, sc_m, gt_m, sh_f, sc_f, gt_f = adaln(c, w_ada_l, b_ada_l)
    h = rms_norm(x, g_mix) * (1.0 + sc_m) + sh_m
    m, s_r, k_s, v_s = token_mixer(h, pos, ret_core, sb_core, w_in_l, b_sb_l, w_ret_br_l, w_sb_br_l, w_o_l)
    x = x + gt_m * m
    h = rms_norm(x, g_ffn) * (1.0 + sc_f) + sh_f
    x = x + gt_f * ffn(h)
    return x, s_r, k_s, v_s


def setup_inputs(seed: int = 0) -> dict:
    key = jax.random.key(seed)
    ks = jax.random.split(key, 27)
    f32 = jnp.float32
    n_pages = PAST_LEN // PAGE_SIZE
    n_used = DEC_BATCH * n_pages
    n_pool = n_used + (n_used + 3) // 4

    def nrm(k, shape, scale):
        return jax.random.normal(k, shape, f32) * scale

    page_table = jax.random.permutation(ks[0], n_pool)[:n_used].reshape(DEC_BATCH, n_pages).astype(jnp.int32)
    return {
        'x_prompt': nrm(ks[1], (BATCH, SEQ, D_MODEL), 1.0),
        'x_sample': nrm(ks[2], (DEC_BATCH, DEC_SEQ, D_MODEL), 1.0),
        'cache_k': nrm(ks[3], (DEPTH, n_pool, PAGE_SIZE, SB_HEADS, SB_HEAD_DIM), 1.0),
        'cache_v': nrm(ks[4], (DEPTH, n_pool, PAGE_SIZE, SB_HEADS, SB_HEAD_DIM), 1.0),
        'state_ret': nrm(ks[5], (DEPTH, DEC_BATCH, RET_HEADS, RET_QK_DIM, RET_V_DIM), 0.5),
        'page_table': page_table,
        'c_prompt': nrm(ks[6], (BATCH, D_MODEL), 1.0),
        'c_sample': nrm(ks[7], (DEC_BATCH, D_MODEL), 1.0),
        'w_ada': nrm(ks[8], (DEPTH, D_MODEL, 6 * D_MODEL), 0.5 * D_MODEL ** -0.5),
        'b_ada': nrm(ks[9], (DEPTH, 6 * D_MODEL), 0.01),
        'norm_mix': 1.0 + nrm(ks[10], (DEPTH, D_MODEL), 0.01),
        'norm_ffn': 1.0 + nrm(ks[11], (DEPTH, D_MODEL), 0.01),
        'w_in': nrm(ks[12], (DEPTH, D_MODEL, IN_COLS), D_MODEL ** -0.5),
        'b_sb': SB_BIAS_INIT + nrm(ks[25], (DEPTH, SB_HEADS), 0.1),
        'w_ret_br': nrm(ks[13], (DEPTH, D_RET_V, D_MODEL), D_RET_V ** -0.5),
        'w_sb_br': nrm(ks[14], (DEPTH, D_SB, D_MODEL), D_SB ** -0.5),
        'w_o': nrm(ks[15], (DEPTH, D_MODEL, D_MODEL), D_MODEL ** -0.5),
        'w_dense_gate': nrm(ks[16], (N_DENSE, D_MODEL, D_FF), D_MODEL ** -0.5),
        'w_dense_up': nrm(ks[17], (N_DENSE, D_MODEL, D_FF), D_MODEL ** -0.5),
        'w_dense_down': nrm(ks[18], (N_DENSE, D_FF, D_MODEL), D_FF ** -0.5),
        'w_router': nrm(ks[19], (N_MOE, D_MODEL, N_EXPERTS), D_MODEL ** -0.5),
        'b_router': nrm(ks[20], (N_MOE, N_EXPERTS), 0.01),
        'w_moe_gate': nrm(ks[21], (N_MOE, N_EXPERTS, D_MODEL, D_FF), D_MODEL ** -0.5),
        'w_moe_up': nrm(ks[22], (N_MOE, N_EXPERTS, D_MODEL, D_FF), D_MODEL ** -0.5),
        'w_moe_down': nrm(ks[23], (N_MOE, N_EXPERTS, D_FF, D_MODEL), D_FF ** -0.5),
        'norm_final': 1.0 + nrm(ks[24], (D_MODEL,), 0.01),
    }


def reference(x_prompt, x_sample, cache_k, cache_v, state_ret, page_table, c_prompt, c_sample,
              w_ada, b_ada, norm_mix, norm_ffn, w_in, b_sb, w_ret_br, w_sb_br, w_o,
              w_dense_gate, w_dense_up, w_dense_down, w_router, b_router,
              w_moe_gate, w_moe_up, w_moe_down, norm_final):
    seq = x_prompt.shape[1]
    dec_b, dec_s = x_sample.shape[0], x_sample.shape[1]
    n_pages = page_table.shape[1]
    past_len = n_pages * cache_k.shape[2]
    pos_p = jnp.arange(seq, dtype=jnp.int32)
    pos_s = past_len + jnp.arange(dec_s, dtype=jnp.int32)
    log_g = retention_log_decay()
    ret_core_p = functools.partial(retention_prompt, log_g=log_g)

    xp, xs = x_prompt, x_sample
    kp_l, vp_l, sp_l, ks_l, vs_l, ss_l = [], [], [], [], [], []
    for l in range(DEPTH):
        i = l // 2
        if l % 2 == 0:
            ffn = functools.partial(swiglu, w_gate=w_dense_gate[i], w_up=w_dense_up[i], w_down=w_dense_down[i])
        else:
            ffn = functools.partial(moe_swiglu, w_router=w_router[i], b_router=b_router[i],
                                    w_gate=w_moe_gate[i], w_up=w_moe_up[i], w_down=w_moe_down[i])
        shared = (w_ada[l], b_ada[l], norm_mix[l], norm_ffn[l], w_in[l], b_sb[l],
                  w_ret_br[l], w_sb_br[l], w_o[l], ffn)

        xp, s_p, k_p, v_p = decoder_layer(xp, c_prompt, pos_p, ret_core_p, sb_prompt, *shared)

        def ret_core_s(q, k, v, S=state_ret[l]):
            return retention_chunk(q, k, v, S.astype(jnp.float32), log_g)

        def sb_core_s(q, k, v, bias, layer=l):
            k_past = cache_k[layer][page_table].reshape(dec_b, past_len, SB_HEADS, SB_HEAD_DIM)
            v_past = cache_v[layer][page_table].reshape(dec_b, past_len, SB_HEADS, SB_HEAD_DIM)
            return sb_sample(q, k, v, k_past, v_past, pos_s, bias)

        xs, s_s, k_sm, v_sm = decoder_layer(xs, c_sample, pos_s, ret_core_s, sb_core_s, *shared)

        kp_l.append(k_p)
        vp_l.append(v_p)
        sp_l.append(s_p)
        ks_l.append(k_sm)
        vs_l.append(v_sm)
        ss_l.append(s_s)

    y_prompt = rms_norm(xp, norm_final)
    y_sample = rms_norm(xs, norm_final)
    k_prompt = jnp.stack(kp_l).astype(cache_k.dtype)
    v_prompt = jnp.stack(vp_l).astype(cache_v.dtype)
    ret_prompt = jnp.stack(sp_l).astype(state_ret.dtype)
    k_sample = jnp.stack(ks_l).astype(cache_k.dtype)
    v_sample = jnp.stack(vs_l).astype(cache_v.dtype)
    ret_sample = jnp.stack(ss_l).astype(state_ret.dtype)
    return (y_prompt, y_sample, k_prompt, v_prompt, ret_prompt, k_sample, v_sample, ret_sample)
```

```python
import functools
import math

import jax
import jax.numpy as jnp
from jax import lax
from jax.experimental import pallas as pl
from jax.experimental.pallas import tpu as pltpu

F32 = jnp.float32
BF16 = jnp.bfloat16

NORM_EPS = 1e-6
ROPE_BASE = 10000.0
N_EXPERTS = 8
TOP_K = 2
RET_CHUNK = 128
ROW_BLOCK = 128
SB_QUERY_BLOCK = 128
SB_KEY_BLOCK = 128
SB_PAGES_PER_STEP = 4
V7X_VMEM_LIMIT = 56 * 1024 * 1024


def _div_tile(n, cap, mult):
    best = None
    for t in range(mult, min(n, cap) + 1, mult):
        if n % t == 0:
            best = t
    assert best is not None, (n, cap, mult)
    return best


def _cparams(semantics, vmem=V7X_VMEM_LIMIT):
    return pltpu.CompilerParams(dimension_semantics=semantics, vmem_limit_bytes=vmem)


def _nt_dot(a, b):
    return lax.dot_general(a, b, (((1,), (1,)), ((), ())), preferred_element_type=F32)


def _tn_dot(a, b):
    return lax.dot_general(a, b, (((0,), (0,)), ((), ())), preferred_element_type=F32)


def _dot(a, b):
    return jnp.dot(a, b, preferred_element_type=F32)


def _split(x):
    hi = x.astype(BF16)
    return hi, (x - hi.astype(F32)).astype(BF16)


def _dot_hp(a, b, dot=_dot):
    a_hi, a_lo = _split(a)
    b_hi, b_lo = _split(b)
    return dot(a_hi, b_hi) + (dot(a_lo, b_hi) + dot(a_hi, b_lo))


def _adaln_body(c_ref, w_ref, b_ref, o_ref):
    c = c_ref[...]
    o_ref[...] = _dot_hp(c * jax.nn.sigmoid(c), w_ref[...]) + b_ref[...]


def adaln(c_all, w_ada, b_ada, layer, tn=512):
    rows, d = c_all.shape
    n = w_ada.shape[-1]
    b3 = b_ada.reshape(b_ada.shape[0], 1, n)
    return pl.pallas_call(
        _adaln_body,
        grid=(n // tn,),
        in_specs=[pl.BlockSpec((rows, d), lambda j: (0, 0)),
                  pl.BlockSpec((None, d, tn), lambda j: (layer, 0, j)),
                  pl.BlockSpec((None, 1, tn), lambda j: (layer, 0, j))],
        out_specs=pl.BlockSpec((rows, tn), lambda j: (0, j)),
        out_shape=jax.ShapeDtypeStruct((rows, n), F32),
        compiler_params=_cparams(("arbitrary",)),
        name="adaln",
    )(c_all, w_ada, b3)


def _norm_body(*refs, n_seq_blocks, has_resid, write_x, modulate, tail_f32):
    it = iter(refs)
    x_ref = next(it)
    if has_resid:
        m_ref, gts_ref, gtr_ref = next(it), next(it), next(it)
    g_ref = next(it)
    if modulate:
        scs_ref, shs_ref, scr_ref, shr_ref = next(it), next(it), next(it), next(it)
    if write_x:
        xo_ref = next(it)
    h_ref = next(it)
    if tail_f32:
        tail_ref = next(it)
    i = pl.program_id(0)

    def run(gt, sc, sh, is_tail):
        x = x_ref[...]
        if has_resid:
            x = x + gt * m_ref[...]
            if write_x:
                xo_ref[...] = x
        y = x * lax.rsqrt(jnp.mean(x * x, axis=-1, keepdims=True) + NORM_EPS) * g_ref[...]
        if modulate:
            y = y * (1.0 + sc) + sh
        h_ref[...] = y.astype(h_ref.dtype)
        if tail_f32 and is_tail:
            tail_ref[...] = y

    @pl.when(i < n_seq_blocks)
    def _():
        run(gts_ref[...] if has_resid else None,
            scs_ref[...] if modulate else None, shs_ref[...] if modulate else None, False)

    @pl.when(i >= n_seq_blocks)
    def _():
        run(gtr_ref[...] if has_resid else None,
            scr_ref[...] if modulate else None, shr_ref[...] if modulate else None, True)


def norm_mod(x, g, layer, mods, *, seq_len, n_seq, resid=None, modulate=True, write_x=True,
             out_dtype=BF16, tail_f32=False):
    rows, d = x.shape
    tr = ROW_BLOCK
    blocks_per_seq = seq_len // tr
    n_seq_blocks = n_seq * blocks_per_seq
    n_blocks = rows // tr
    mod_seq, mod_rows = mods
    row_spec = pl.BlockSpec((tr, d), lambda i: (i, 0))

    def seq_spec(k):
        return pl.BlockSpec((None, None, 1, d),
                            lambda i: (jnp.minimum(i // blocks_per_seq, n_seq - 1), k, 0, 0))

    def rows_spec(k):
        return pl.BlockSpec((None, tr, d), lambda i: (k, 0, 0))

    args, specs = [x], [row_spec]
    has_resid = resid is not None
    if has_resid:
        m, gt_idx, sc_idx, sh_idx = resid[0], resid[1], resid[2], resid[3]
        args += [m, mod_seq, mod_rows]
        specs += [row_spec, seq_spec(gt_idx), rows_spec(gt_idx)]
    else:
        sc_idx, sh_idx = 1, 0
    args.append(g.reshape(g.shape[0], 1, d) if g.ndim == 2 else g.reshape(1, 1, d))
    specs.append(pl.BlockSpec((None, 1, d), lambda i: (layer if g.ndim == 2 else 0, 0, 0)))
    if modulate:
        args += [mod_seq, mod_seq, mod_rows, mod_rows]
        specs += [seq_spec(sc_idx), seq_spec(sh_idx), rows_spec(sc_idx), rows_spec(sh_idx)]
    out_shape, out_specs = [], []
    if has_resid and write_x:
        out_shape.append(jax.ShapeDtypeStruct((rows, d), F32))
        out_specs.append(row_spec)
    out_shape.append(jax.ShapeDtypeStruct((rows, d), out_dtype))
    out_specs.append(row_spec)
    if tail_f32:
        assert n_blocks == n_seq_blocks + 1
        out_shape.append(jax.ShapeDtypeStruct((tr, d), F32))
        out_specs.append(pl.BlockSpec((tr, d), lambda i: (0, 0)))
    res = pl.pallas_call(
        functools.partial(_norm_body, n_seq_blocks=n_seq_blocks, has_resid=has_resid,
                          write_x=has_resid and write_x, modulate=modulate, tail_f32=tail_f32),
        grid=(n_blocks,),
        in_specs=specs,
        out_specs=out_specs,
        out_shape=out_shape,
        compiler_params=_cparams(("arbitrary",)),
        name="norm_mod",
    )(*args)
    return res if len(res) > 1 else res[0]


def _tail_dot(t_ref, whi_ref, wlo_ref):
    t_hi, t_lo = _split(t_ref[...])
    return _dot(t_hi, whi_ref[...]) + (_dot(t_lo, whi_ref[...]) + _dot(t_hi, wlo_ref[...]))


def _mm_body(x_ref, t_ref, w_ref, o_ref, whi_ref, wlo_ref):
    i = pl.program_id(1)

    @pl.when(i == 0)
    def _():
        hi, lo = _split(w_ref[...])
        whi_ref[...] = hi
        wlo_ref[...] = lo

    o_ref[...] = _dot(x_ref[...], whi_ref[...])

    @pl.when(i == pl.num_programs(1) - 1)
    def _():
        tr = t_ref.shape[0]
        o_ref[o_ref.shape[0] - tr:, :] = _tail_dot(t_ref, whi_ref, wlo_ref)


def matmul(x, tail, w, layer, *, tm, tn):
    m, k = x.shape
    n = w.shape[-1]
    tr = tail.shape[0]
    return pl.pallas_call(
        _mm_body,
        grid=(n // tn, m // tm),
        in_specs=[pl.BlockSpec((tm, k), lambda j, i: (i, 0)),
                  pl.BlockSpec((tr, k), lambda j, i: (0, 0)),
                  pl.BlockSpec((None, k, tn), lambda j, i: (layer, 0, j))],
        out_specs=pl.BlockSpec((tm, tn), lambda j, i: (i, j)),
        out_shape=jax.ShapeDtypeStruct((m, n), F32),
        scratch_shapes=[pltpu.VMEM((k, tn), BF16), pltpu.VMEM((k, tn), BF16)],
        compiler_params=_cparams(("arbitrary", "arbitrary")),
        name="matmul",
    )(x, tail, w)


def _merge_body(r_ref, s_ref, rt_ref, st_ref, wr_ref, ws_ref, ar_ref, as_ref, o_ref, ot_ref,
                wrhi_ref, wrlo_ref, wshi_ref, wslo_ref):
    i = pl.program_id(1)

    @pl.when(i == 0)
    def _():
        wrhi_ref[...], wrlo_ref[...] = _split(wr_ref[...])
        wshi_ref[...], wslo_ref[...] = _split(ws_ref[...])

    gate_r = jax.nn.sigmoid(ar_ref[...])
    gate_s = jax.nn.sigmoid(as_ref[...])
    u = gate_r * _dot(r_ref[...], wrhi_ref[...]) + gate_s * _dot(s_ref[...], wshi_ref[...])
    o_ref[...] = u.astype(o_ref.dtype)

    @pl.when(i == pl.num_programs(1) - 1)
    def _():
        tr = rt_ref.shape[0]
        lo = o_ref.shape[0] - tr
        ut = (gate_r[lo:, :] * _tail_dot(rt_ref, wrhi_ref, wrlo_ref)
              + gate_s[lo:, :] * _tail_dot(st_ref, wshi_ref, wslo_ref))
        ot_ref[...] = ut
        o_ref[lo:, :] = ut.astype(o_ref.dtype)


def branch_merge(r, s, r_tail, s_tail, w_ret_br, w_sb_br, proj, layer, *, ar_col, as_col, tm, tn):
    m, kr = r.shape
    ks = s.shape[1]
    n = w_ret_br.shape[-1]
    tr = r_tail.shape[0]
    return pl.pallas_call(
        _merge_body,
        grid=(n // tn, m // tm),
        in_specs=[pl.BlockSpec((tm, kr), lambda j, i: (i, 0)),
                  pl.BlockSpec((tm, ks), lambda j, i: (i, 0)),
                  pl.BlockSpec((tr, kr), lambda j, i: (0, 0)),
                  pl.BlockSpec((tr, ks), lambda j, i: (0, 0)),
                  pl.BlockSpec((None, kr, tn), lambda j, i: (layer, 0, j)),
                  pl.BlockSpec((None, ks, tn), lambda j, i: (layer, 0, j)),
                  pl.BlockSpec((tm, tn), lambda j, i: (i, ar_col // tn + j)),
                  pl.BlockSpec((tm, tn), lambda j, i: (i, as_col // tn + j))],
        out_specs=[pl.BlockSpec((tm, tn), lambda j, i: (i, j)),
                   pl.BlockSpec((tr, tn), lambda j, i: (0, j))],
        out_shape=[jax.ShapeDtypeStruct((m, n), BF16), jax.ShapeDtypeStruct((tr, n), F32)],
        scratch_shapes=[pltpu.VMEM((kr, tn), BF16), pltpu.VMEM((kr, tn), BF16),
                        pltpu.VMEM((ks, tn), BF16), pltpu.VMEM((ks, tn), BF16)],
        compiler_params=_cparams(("arbitrary", "arbitrary")),
        name="branch_merge",
    )(r, s, r_tail, s_tail, w_ret_br, w_sb_br, proj, proj)


def _ret_tables(n_heads, chunk):
    log_g = jnp.log1p(-jnp.exp2(-5.0 - jnp.arange(n_heads, dtype=F32)))
    idx = jnp.arange(chunk, dtype=F32)
    diff = idx[:, None] - idx[None, :]
    decay = jnp.where(diff >= 0, jnp.exp(jnp.maximum(diff, 0.0)[None] * log_g[:, None, None]), 0.0)
    q_decay = jnp.exp((idx[None, :] + 1.0) * log_g[:, None])[:, :, None]
    k_decay = jnp.exp((chunk - 1.0 - idx)[None, :] * log_g[:, None])[:, :, None]
    c_decay = jnp.exp(chunk * log_g)[:, None, None]
    return decay, q_decay, k_decay, c_decay


def _rope_tables(pos, half):
    inv_freq = jnp.power(ROPE_BASE, -jnp.linspace(0.0, 1.0, half, dtype=F32))
    ang = pos.astype(F32)[:, None] * inv_freq[None, :]
    return jnp.cos(ang), jnp.sin(ang)


def _ret_chunk(q, k, v, g, state, cos, sin, dec, qd, kd, cd, precise=False):
    half = cos.shape[-1]

    def rot(x):
        x1, x2 = x[:, :half], x[:, half:]
        return jnp.concatenate([x1 * cos - x2 * sin, x2 * cos + x1 * sin], axis=-1)

    dk = q.shape[-1]
    qr = rot(q)
    kr = rot(k) * dk ** -0.5
    if precise:
        scores = _dot_hp(qr, kr, _nt_dot) * dec
        o = _dot_hp(scores, v) + _dot_hp(qr, state) * qd
        new_state = cd * state + _dot_hp(kr * kd, v, _tn_dot)
    else:
        qb, kb, vb = qr.astype(BF16), kr.astype(BF16), v.astype(BF16)
        scores = _nt_dot(qb, kb) * dec
        o = _dot(scores.astype(BF16), vb) + _dot(qb, state.astype(BF16)) * qd
        new_state = cd * state + _tn_dot((kr * kd).astype(BF16), vb)
    o = o * lax.rsqrt(jnp.mean(o * o, axis=-1, keepdims=True) + NORM_EPS)
    return o * (g * jax.nn.sigmoid(g)), new_state


def _ret_prompt_body(q_ref, k_ref, v_ref, g_ref, cos_ref, sin_ref, dec_ref, qd_ref, kd_ref, cd_ref,
                     r_ref, s_ref, state_ref):
    @pl.when(pl.program_id(2) == 0)
    def _():
        state_ref[...] = jnp.zeros_like(state_ref)

    r, new_state = _ret_chunk(q_ref[...], k_ref[...], v_ref[...], g_ref[...], state_ref[...],
                              cos_ref[...], sin_ref[...], dec_ref[...], qd_ref[...], kd_ref[...],
                              cd_ref[...])
    r_ref[...] = r.astype(r_ref.dtype)
    state_ref[...] = new_state
    s_ref[...] = new_state


def retention_prompt(proj, *, n_seq, seq_len, n_heads, dk, dv, cols):
    c = RET_CHUNK
    n_chunks = seq_len // c
    dec, qd, kd, cd = _ret_tables(n_heads, c)
    cos, sin = _rope_tables(jnp.arange(seq_len, dtype=jnp.int32), dk // 2)

    def col_spec(col0, width):
        return pl.BlockSpec((c, width), lambda b, h, t: (b * n_chunks + t, col0 // width + h))

    head3 = lambda shape: pl.BlockSpec((None,) + shape, lambda b, h, t: (h, 0, 0))
    rope_spec = pl.BlockSpec((c, dk // 2), lambda b, h, t: (t, 0))
    return pl.pallas_call(
        _ret_prompt_body,
        grid=(n_seq, n_heads, n_chunks),
        in_specs=[col_spec(cols[0], dk), col_spec(cols[1], dk), col_spec(cols[2], dv),
                  col_spec(cols[3], dv), rope_spec, rope_spec,
                  head3((c, c)), head3((c, 1)), head3((c, 1)), head3((1, 1))],
        out_specs=[pl.BlockSpec((c, dv), lambda b, h, t: (b * n_chunks + t, h)),
                   pl.BlockSpec((None, None, dk, dv), lambda b, h, t: (b, h, 0, 0))],
        out_shape=[jax.ShapeDtypeStruct((n_seq * seq_len, n_heads * dv), BF16),
                   jax.ShapeDtypeStruct((n_seq, n_heads, dk, dv), F32)],
        scratch_shapes=[pltpu.VMEM((dk, dv), F32)],
        compiler_params=_cparams(("arbitrary", "arbitrary", "arbitrary")),
        name="retention_prompt",
    )(proj, proj, proj, proj, cos, sin, dec, qd, kd, cd)


def _ret_sample_body(q_ref, k_ref, v_ref, g_ref, cos_ref, sin_ref, dec_ref, qd_ref, kd_ref, cd_ref,
                     s0_ref, r_ref, s_ref, *, n_seq, chunk):
    for b in range(n_seq):
        rows = slice(b * chunk, (b + 1) * chunk)
        r, new_state = _ret_chunk(q_ref[rows, :], k_ref[rows, :], v_ref[rows, :], g_ref[rows, :],
                                  s0_ref[b], cos_ref[...], sin_ref[...], dec_ref[...], qd_ref[...],
                                  kd_ref[...], cd_ref[...], precise=True)
        r_ref[rows, :] = r
        s_ref[b] = new_state


def retention_sample(proj, state, layer, *, row0, n_seq, chunk, past_len, n_heads, dk, dv, cols):
    rows = n_seq * chunk
    dec, qd, kd, cd = _ret_tables(n_heads, chunk)
    cos, sin = _rope_tables(past_len + jnp.arange(chunk, dtype=jnp.int32), dk // 2)

    def col_spec(col0, width):
        return pl.BlockSpec((rows, width), lambda h: (row0 // rows, col0 // width + h))

    head3 = lambda shape: pl.BlockSpec((None,) + shape, lambda h: (h, 0, 0))
    rope_spec = pl.BlockSpec((chunk, dk // 2), lambda h: (0, 0))
    state_spec = pl.BlockSpec((None, n_seq, None, dk, dv), lambda h: (layer, 0, h, 0, 0))
    return pl.pallas_call(
        functools.partial(_ret_sample_body, n_seq=n_seq, chunk=chunk),
        grid=(n_heads,),
        in_specs=[col_spec(cols[0], dk), col_spec(cols[1], dk), col_spec(cols[2], dv),
                  col_spec(cols[3], dv), rope_spec, rope_spec,
                  head3((chunk, chunk)), head3((chunk, 1)), head3((chunk, 1)), head3((1, 1)),
                  state_spec],
        out_specs=[pl.BlockSpec((rows, dv), lambda h: (0, h)),
                   pl.BlockSpec((n_seq, None, dk, dv), lambda h: (0, h, 0, 0))],
        out_shape=[jax.ShapeDtypeStruct((rows, n_heads * dv), F32),
                   jax.ShapeDtypeStruct((n_seq, n_heads, dk, dv), F32)],
        compiler_params=_cparams(("arbitrary",)),
        name="retention_sample",
    )(proj, proj, proj, proj, cos, sin, dec, qd, kd, cd, state)


def _strict_upper_ones(n):
    j = lax.broadcasted_iota(jnp.int32, (n, n), 0)
    s = lax.broadcasted_iota(jnp.int32, (n, n), 1)
    return (j > s).astype(BF16)


def _sb_block(z, mask, run, tri):
    t = jnp.log1p(jnp.exp(-jnp.abs(z)))
    log_sig = jnp.minimum(z, 0.0) - t
    log_not = -jnp.maximum(z, 0.0) - t
    if mask is not None:
        log_not = jnp.where(mask, log_not, 0.0)
    hi = log_not.astype(BF16)
    lo = (log_not - hi.astype(F32)).astype(BF16)
    excl = _dot(hi, tri) + _dot(lo, tri) + run
    w = jnp.exp(log_sig + excl)
    if mask is not None:
        w = jnp.where(mask, w, 0.0)
    return w, run + jnp.sum(log_not, axis=-1, keepdims=True)


def _sb_prompt_body(bias_ref, q_ref, k_ref, v_ref, tri_ref, o_ref, *, scale):
    h = pl.program_id(1)
    i = pl.program_id(2)
    tq, d = q_ref.shape
    tk = SB_KEY_BLOCK
    bias = bias_ref[h]
    q = q_ref[...].astype(BF16)
    tri = tri_ref[...]
    qpos = i * tq + lax.broadcasted_iota(jnp.int32, (tq, tk), 0)
    n_key_blocks = (i + 1) * (tq // tk)

    def step(jj, carry):
        run, acc = carry
        k0 = pl.multiple_of((n_key_blocks - 1 - jj) * tk, tk)
        k = k_ref[pl.ds(k0, tk), :].astype(BF16)
        v = v_ref[pl.ds(k0, tk), :].astype(BF16)
        z = _nt_dot(q, k) * scale + bias
        mask = (k0 + lax.broadcasted_iota(jnp.int32, (tq, tk), 1)) < qpos
        w, run = _sb_block(z, mask, run, tri)
        return run, acc + _dot(w.astype(BF16), v)

    _, acc = lax.fori_loop(0, n_key_blocks, step,
                           (jnp.zeros((tq, 1), F32), jnp.zeros((tq, d), F32)))
    o_ref[...] = acc.astype(o_ref.dtype)


def sb_prompt(proj, bias, *, n_seq, seq_len, n_heads, d, cols):
    tq = SB_QUERY_BLOCK
    nq = seq_len // tq
    tri = _strict_upper_ones(SB_KEY_BLOCK)
    kv_spec = lambda col0: pl.BlockSpec((seq_len, d), lambda b, h, i, bias: (b, col0 // d + h))
    grid_spec = pltpu.PrefetchScalarGridSpec(
        num_scalar_prefetch=1,
        grid=(n_seq, n_heads, nq),
        in_specs=[pl.BlockSpec((tq, d), lambda b, h, i, bias: (b * nq + i, cols[0] // d + h)),
                  kv_spec(cols[1]), kv_spec(cols[2]),
                  pl.BlockSpec((SB_KEY_BLOCK, SB_KEY_BLOCK), lambda b, h, i, bias: (0, 0))],
        out_specs=pl.BlockSpec((tq, d), lambda b, h, i, bias: (b * nq + i, h)),
    )
    return pl.pallas_call(
        functools.partial(_sb_prompt_body, scale=d ** -0.5),
        grid_spec=grid_spec,
        out_shape=jax.ShapeDtypeStruct((n_seq * seq_len, n_heads * d), BF16),
        compiler_params=_cparams(("arbitrary", "arbitrary", "arbitrary")),
        name="sb_prompt",
    )(bias, proj, proj, proj, tri)


def _stack_hi_lo(x):
    hi = x.astype(BF16).astype(F32)
    return jnp.concatenate([hi, x - hi], axis=0).astype(BF16)


def _stacked_dot(a2, b, dot):
    b_hi, b_lo = _split(b)
    n = a2.shape[0] // 2
    p, r = dot(a2, b_hi), dot(a2, b_lo)
    return p[:n] + (p[n:] + r[:n])


def _sb_sample_body(pt_ref, q_ref, kn_ref, vn_ref, bias_ref, tri_ref, *rest,
                    scale, n_heads, n_q, d, pages_per_step):
    k_refs = rest[:pages_per_step]
    v_refs = rest[pages_per_step:2 * pages_per_step]
    o_ref, q2_ref, run_ref, acc_ref = rest[2 * pages_per_step:]
    step = pl.program_id(1)
    rows = n_heads * n_q
    page = tri_ref.shape[0]
    tri = tri_ref[...]
    bias = bias_ref[...]

    def attend(k_of_head, v_of_head, mask):
        z = jnp.concatenate([_stacked_dot(q2_ref[h], k_of_head(h), _nt_dot) for h in range(n_heads)],
                            axis=0) * scale + bias
        w, run = _sb_block(z, mask, run_ref[...], tri)
        run_ref[...] = run
        for h in range(n_heads):
            head_rows = slice(h * n_q, (h + 1) * n_q)
            acc_ref[head_rows, :] += _stacked_dot(_stack_hi_lo(w[head_rows, :]), v_of_head(h), _dot)

    @pl.when(step == 0)
    def _():
        for h in range(n_heads):
            q2_ref[h] = _stack_hi_lo(q_ref[:, h * d:(h + 1) * d])
        run_ref[...] = jnp.zeros_like(run_ref)
        acc_ref[...] = jnp.zeros_like(acc_ref)
        pad = jnp.zeros((page - n_q, d), F32)
        q_idx = jnp.concatenate([lax.broadcasted_iota(jnp.int32, (n_q, page), 0)] * n_heads, axis=0)
        key_idx = lax.broadcasted_iota(jnp.int32, (rows, page), 1)
        attend(lambda h: jnp.concatenate([kn_ref[:, h * d:(h + 1) * d], pad], axis=0),
               lambda h: jnp.concatenate([vn_ref[:, h * d:(h + 1) * d], pad], axis=0),
               key_idx < q_idx)

    for u in range(pages_per_step):
        attend(lambda h, ref=k_refs[u]: ref[pl.ds(h, page, stride=n_heads), :],
               lambda h, ref=v_refs[u]: ref[pl.ds(h, page, stride=n_heads), :], None)

    @pl.when(step == pl.num_programs(1) - 1)
    def _():
        for h in range(n_heads):
            o_ref[:, h * d:(h + 1) * d] = acc_ref[h * n_q:(h + 1) * n_q, :]


def sb_sample(proj, cache_k, cache_v, page_table, bias, layer, *, row0, n_seq, n_q, n_heads, d, cols):
    depth, n_pool, page = cache_k.shape[0], cache_k.shape[1], cache_k.shape[2]
    n_pages = page_table.shape[1]
    width = n_heads * d
    ck = cache_k.reshape(depth, n_pool, page * n_heads, d)
    cv = cache_v.reshape(depth, n_pool, page * n_heads, d)
    pps = SB_PAGES_PER_STEP
    n_steps = n_pages // pps
    rows = n_heads * n_q
    bias_col = jnp.repeat(bias.astype(F32), n_q)[:, None]
    tri = _strict_upper_ones(page)

    def new_spec(col0):
        return pl.BlockSpec((n_q, width), lambda b, s, pt: (row0 // n_q + b, col0 // width))

    def page_spec(u):
        return pl.BlockSpec((None, None, page * n_heads, d),
                            lambda b, s, pt: (layer, pt[b, n_pages - 1 - (s * pps + u)], 0, 0))

    grid_spec = pltpu.PrefetchScalarGridSpec(
        num_scalar_prefetch=1,
        grid=(n_seq, n_steps),
        in_specs=[new_spec(cols[0]), new_spec(cols[1]), new_spec(cols[2]),
                  pl.BlockSpec((rows, 1), lambda b, s, pt: (0, 0)),
                  pl.BlockSpec((page, page), lambda b, s, pt: (0, 0))]
                 + [page_spec(u) for u in range(pps)] * 2,
        out_specs=pl.BlockSpec((n_q, width), lambda b, s, pt: (b, 0)),
        scratch_shapes=[pltpu.VMEM((n_heads, 2 * n_q, d), BF16), pltpu.VMEM((rows, 1), F32),
                        pltpu.VMEM((rows, d), F32)],
    )
    return pl.pallas_call(
        functools.partial(_sb_sample_body, scale=d ** -0.5, n_heads=n_heads, n_q=n_q, d=d,
                          pages_per_step=pps),
        grid_spec=grid_spec,
        out_shape=jax.ShapeDtypeStruct((n_seq * n_q, width), F32),
        compiler_params=_cparams(("arbitrary", "arbitrary")),
        name="sb_sample",
    )(page_table, proj, proj, proj, bias_col, tri, *([ck] * pps), *([cv] * pps))


def _mlp_body(tok_ref, texp_ref, tact_ref, h_hbm, wg_ref, wu_ref, wd_ref, y_hbm,
              x_ref, land_ref, acc_ref, sem_in, sem_out, *tail_refs, tm, ch, tail_row0):
    del texp_ref
    i = pl.program_id(0)
    f = pl.program_id(1)
    nf = pl.num_programs(1)
    active = tact_ref[i] > 0
    d = acc_ref.shape[1]
    tn = _div_tile(d, 512, 128)

    def row_copy(tok, r):
        return pltpu.make_async_copy(h_hbm.at[pl.ds(tok, 1)], land_ref.at[pl.ds(r, 1)], sem_in)

    @pl.when(jnp.logical_and(active, f == 0))
    def _():
        acc_ref[...] = jnp.zeros_like(acc_ref)
        for c in range(tm // ch):
            base = i * tm + c * ch

            def issue(r, carry):
                row_copy(tok_ref[base + r], r).start()
                return carry

            def wait(r, carry):
                row_copy(0, r).wait()
                return carry

            lax.fori_loop(0, ch, issue, 0)
            lax.fori_loop(0, ch, wait, 0)
            x_ref[c * ch:(c + 1) * ch, :] = land_ref[...].astype(BF16)

    @pl.when(active)
    def _():
        x = x_ref[...]
        g = _dot(x, wg_ref[...].astype(BF16))
        u = _dot(x, wu_ref[...].astype(BF16))
        a = (g * jax.nn.sigmoid(g) * u).astype(BF16)
        for n in range(d // tn):
            cols = slice(n * tn, (n + 1) * tn)
            acc_ref[:, cols] += _dot(a, wd_ref[:, cols].astype(BF16))

    @pl.when(jnp.logical_and(jnp.logical_not(active), f == 0))
    def _():
        acc_ref[...] = jnp.zeros_like(acc_ref)

    if tail_row0 is not None:
        xt_ref, acct_ref, sem_t = tail_refs
        tr = xt_ref.shape[0]
        tail_tile, tail_off = divmod(tail_row0, tm)
        in_tail_tile = i == tail_tile

        @pl.when(jnp.logical_and(in_tail_tile, f == 0))
        def _():
            load = pltpu.make_async_copy(h_hbm.at[pl.ds(tail_row0, tr)], xt_ref, sem_t)
            load.start()
            load.wait()
            acct_ref[...] = jnp.zeros_like(acct_ref)

        @pl.when(in_tail_tile)
        def _():
            xt = xt_ref[...]
            g = _dot_hp(xt, wg_ref[...])
            u = _dot_hp(xt, wu_ref[...])
            a = g * jax.nn.sigmoid(g) * u
            for n in range(d // tn):
                cols = slice(n * tn, (n + 1) * tn)
                acct_ref[:, cols] += _dot_hp(a, wd_ref[:, cols])

        @pl.when(jnp.logical_and(in_tail_tile, f == nf - 1))
        def _():
            acc_ref[tail_off:tail_off + tr, :] = acct_ref[...]

    @pl.when(f == nf - 1)
    def _():
        out = pltpu.make_async_copy(acc_ref, y_hbm.at[pl.ds(i * tm, tm)], sem_out)
        out.start()
        out.wait()


def fused_mlp(h, w_gate, w_up, w_down, tok_of_row, tile_expert, tile_active, *, tm, tf, tail_row0=None):
    d = h.shape[1]
    ff = w_gate.shape[-1]
    n_tiles = tile_expert.shape[0]
    nf = ff // tf
    ch = _div_tile(tm, 256, 16)
    tail_scratch = []
    if tail_row0 is not None:
        assert tail_row0 % tm + ROW_BLOCK <= tm
        tail_scratch = [pltpu.VMEM((ROW_BLOCK, d), F32), pltpu.VMEM((ROW_BLOCK, d), F32),
                        pltpu.SemaphoreType.DMA]

    def f_idx(i, f, tact):
        return jnp.where(tact[i] > 0, f, nf - 1)

    grid_spec = pltpu.PrefetchScalarGridSpec(
        num_scalar_prefetch=3,
        grid=(n_tiles, nf),
        in_specs=[pl.BlockSpec(memory_space=pl.ANY),
                  pl.BlockSpec((None, d, tf), lambda i, f, tok, texp, tact: (texp[i], 0, f_idx(i, f, tact))),
                  pl.BlockSpec((None, d, tf), lambda i, f, tok, texp, tact: (texp[i], 0, f_idx(i, f, tact))),
                  pl.BlockSpec((None, tf, d), lambda i, f, tok, texp, tact: (texp[i], f_idx(i, f, tact), 0))],
        out_specs=pl.BlockSpec(memory_space=pl.ANY),
        scratch_shapes=[pltpu.VMEM((tm, d), BF16), pltpu.VMEM((ch, d), F32), pltpu.VMEM((tm, d), F32),
                        pltpu.SemaphoreType.DMA, pltpu.SemaphoreType.DMA] + tail_scratch,
    )
    return pl.pallas_call(
        functools.partial(_mlp_body, tm=tm, ch=ch, tail_row0=tail_row0),
        grid_spec=grid_spec,
        out_shape=jax.ShapeDtypeStruct((n_tiles * tm, d), F32),
        compiler_params=_cparams(("arbitrary", "arbitrary")),
        name="fused_mlp",
    )(tok_of_row, tile_expert, tile_active, h, w_gate, w_up, w_down)


def _router_body(h_ref, w_ref, b_ref, o_ref, *, n_experts):
    logits = _dot_hp(h_ref[...], w_ref[...]) + b_ref[...]
    lane = lax.broadcasted_iota(jnp.int32, logits.shape, 1).astype(F32)
    neg = jnp.float32(-jnp.inf)
    none = jnp.float32(logits.shape[1])
    logits = jnp.where(lane < n_experts, logits, neg)
    v1 = jnp.max(logits, axis=-1, keepdims=True)
    i1 = jnp.min(jnp.where(logits == v1, lane, none), axis=-1, keepdims=True)
    rest = jnp.where(lane == i1, neg, logits)
    v2 = jnp.max(rest, axis=-1, keepdims=True)
    i2 = jnp.min(jnp.where(rest == v2, lane, none), axis=-1, keepdims=True)
    e2 = jnp.exp(v2 - v1)
    denom = 1.0 + e2
    out = jnp.where(lane == 0, i1,
                    jnp.where(lane == 1, i2,
                              jnp.where(lane == 2, 1.0 / denom,
                                        jnp.where(lane == 3, e2 / denom, 0.0))))
    o_ref[...] = out


def router(h, w_router, b_router, layer):
    rows, d = h.shape
    n_experts = w_router.shape[-1]
    lanes = 128
    w_pad = jnp.pad(w_router, ((0, 0), (0, 0), (0, lanes - n_experts)))
    b_pad = jnp.pad(b_router, ((0, 0), (0, lanes - n_experts))).reshape(-1, 1, lanes)
    tr = ROW_BLOCK
    return pl.pallas_call(
        functools.partial(_router_body, n_experts=n_experts),
        grid=(rows // tr,),
        in_specs=[pl.BlockSpec((tr, d), lambda i: (i, 0)),
                  pl.BlockSpec((None, d, lanes), lambda i: (layer, 0, 0)),
                  pl.BlockSpec((None, 1, lanes), lambda i: (layer, 0, 0))],
        out_specs=pl.BlockSpec((tr, lanes), lambda i: (i, 0)),
        out_shape=jax.ShapeDtypeStruct((rows, lanes), F32),
        compiler_params=_cparams(("arbitrary",)),
        name="router",
    )(h, w_pad, b_pad)


def _combine_body(p0_ref, p1_ref, y_hbm, w0_ref, w1_ref, o_ref, buf0_ref, buf1_ref, sem):
    i = pl.program_id(0)
    tt = o_ref.shape[0]

    def row_copy(src_row, buf_ref, r):
        return pltpu.make_async_copy(y_hbm.at[pl.ds(src_row, 1)], buf_ref.at[pl.ds(r, 1)], sem)

    def issue(r, carry):
        row_copy(p0_ref[i * tt + r], buf0_ref, r).start()
        row_copy(p1_ref[i * tt + r], buf1_ref, r).start()
        return carry

    def wait(r, carry):
        row_copy(0, buf0_ref, r).wait()
        row_copy(0, buf1_ref, r).wait()
        return carry

    lax.fori_loop(0, tt, issue, 0)
    lax.fori_loop(0, tt, wait, 0)
    o_ref[...] = w0_ref[...] * buf0_ref[...] + w1_ref[...] * buf1_ref[...]


def combine(y, p0, p1, w0, w1):
    rows = p0.shape[0]
    d = y.shape[1]
    tt = ROW_BLOCK
    grid_spec = pltpu.PrefetchScalarGridSpec(
        num_scalar_prefetch=2,
        grid=(rows // tt,),
        in_specs=[pl.BlockSpec(memory_space=pl.ANY),
                  pl.BlockSpec((tt, 1), lambda i, p0, p1: (i, 0)),
                  pl.BlockSpec((tt, 1), lambda i, p0, p1: (i, 0))],
        out_specs=pl.BlockSpec((tt, d), lambda i, p0, p1: (i, 0)),
        scratch_shapes=[pltpu.VMEM((tt, d), F32), pltpu.VMEM((tt, d), F32), pltpu.SemaphoreType.DMA],
    )
    return pl.pallas_call(
        _combine_body,
        grid_spec=grid_spec,
        out_shape=jax.ShapeDtypeStruct((rows, d), F32),
        compiler_params=_cparams(("arbitrary",)),
        name="combine",
    )(p0, p1, y, w0[:, None], w1[:, None])


def _moe_plan(route, n_tokens, n_rows, tm, n_tiles):
    e = route[:n_tokens, :TOP_K].astype(jnp.int32)
    w = route[:n_tokens, TOP_K:2 * TOP_K]
    onehot = (e[:, :, None] == jnp.arange(N_EXPERTS)[None, None, :]).astype(jnp.int32)
    per_tok = onehot.sum(axis=1)
    before = jnp.cumsum(per_tok, axis=0) - per_tok
    counts = per_tok.sum(axis=0)
    tiles_per = (counts + tm - 1) // tm
    tile_end = jnp.cumsum(tiles_per)
    tile_start = tile_end - tiles_per
    rank = jnp.take_along_axis(before, e, axis=1)
    pos = tile_start[e] * tm + rank
    tok = jnp.arange(n_tokens, dtype=jnp.int32)
    tok_of_row = jnp.zeros((n_tiles * tm,), jnp.int32)
    tok_of_row = tok_of_row.at[pos[:, 0]].set(tok).at[pos[:, 1]].set(tok)
    tile_ids = jnp.arange(n_tiles, dtype=jnp.int32)
    total = tile_end[-1]
    tile_expert = jnp.searchsorted(tile_end, jnp.minimum(tile_ids, total - 1), side="right")
    tile_expert = jnp.minimum(tile_expert, N_EXPERTS - 1).astype(jnp.int32)
    tile_active = (tile_ids < total).astype(jnp.int32)
    pad = n_rows - n_tokens
    pos = jnp.pad(pos, ((0, pad), (0, 0))).astype(jnp.int32)
    w = jnp.pad(w, ((0, pad), (0, 0)))
    return tok_of_row, tile_expert, tile_active, pos, w


def kernel(x_prompt, x_sample, cache_k, cache_v, state_ret, page_table, c_prompt, c_sample, w_ada, b_ada, norm_mix, norm_ffn, w_in, b_sb, w_ret_br, w_sb_br, w_o, w_dense_gate, w_dense_up, w_dense_down, w_router, b_router, w_moe_gate, w_moe_up, w_moe_down, norm_final):
    n_seq, seq_len, d_model = x_prompt.shape
    dec_b, dec_s, _ = x_sample.shape
    depth = w_in.shape[0]
    ret_heads, dk, dv = state_ret.shape[2], state_ret.shape[3], state_ret.shape[4]
    sb_heads, sb_d = cache_k.shape[3], cache_k.shape[4]
    past_len = page_table.shape[1] * cache_k.shape[2]
    d_rqk, d_rv, d_sb = ret_heads * dk, ret_heads * dv, sb_heads * sb_d
    ff = w_dense_gate.shape[-1]

    n_prompt = n_seq * seq_len
    n_sample = dec_b * dec_s
    n_tok = n_prompt + n_sample
    assert n_sample <= ROW_BLOCK and seq_len % ROW_BLOCK == 0
    n_rows = n_prompt + ROW_BLOCK

    col = {}
    acc = 0
    for name, size in (("rq", d_rqk), ("rk", d_rqk), ("rv", d_rv), ("rg", d_rv),
                       ("sq", d_sb), ("sk", d_sb), ("sv", d_sb), ("ar", d_model), ("as", d_model)):
        col[name] = acc
        acc += size

    mm_tm = _div_tile(n_rows, 768, 16)
    in_tn = _div_tile(w_in.shape[-1], 512, 128)
    out_tn = _div_tile(d_model, 512, 128)
    merge_tn = _div_tile(math.gcd(d_model, col["ar"], col["as"]), 512, 128)
    mlp_tf = _div_tile(ff, 128, 128)
    dense_tm = _div_tile(n_rows, 1088, 16)
    moe_tm = -(-(TOP_K * n_tok * 11) // (N_EXPERTS * 10 * 2) // 128) * 128

    x = jnp.concatenate([x_prompt.reshape(n_prompt, d_model), x_sample.reshape(n_sample, d_model),
                         jnp.zeros((n_rows - n_tok, d_model), F32)], axis=0)
    c_all = jnp.concatenate([c_prompt, c_sample,
                             jnp.zeros((-(n_seq + dec_b) % 8, d_model), F32)], axis=0)

    def mods_of(layer):
        mod = adaln(c_all, w_ada, b_ada, layer)
        mod_seq = mod.reshape(mod.shape[0], 6, 1, d_model)
        rows_mod = jnp.repeat(mod[n_seq:n_seq + dec_b].reshape(dec_b, 6, d_model), dec_s, axis=0)
        rows_mod = jnp.pad(rows_mod, ((0, ROW_BLOCK - n_sample), (0, 0), (0, 0)))
        return mod_seq, rows_mod.transpose(1, 0, 2)

    norm_kw = dict(seq_len=seq_len, n_seq=n_seq)
    SH_M, SC_M, GT_M, SH_F, SC_F, GT_F = range(6)

    outs = {k: [] for k in ("kp", "vp", "sp", "ks", "vs", "ss")}
    pending = None
    for layer in range(depth):
        mods = mods_of(layer)
        if pending is None:
            h, h_tail = norm_mod(x, norm_mix, layer, mods, tail_f32=True, **norm_kw)
        else:
            m_prev, gt_prev, mods_prev = pending
            x, h, h_tail = _resid_then_norm(x, m_prev, gt_prev, mods_prev, norm_mix, layer, mods,
                                            SC_M, SH_M, norm_kw)
        proj = matmul(h, h_tail, w_in, layer, tm=mm_tm, tn=in_tn)

        r_p, s_p = retention_prompt(proj, n_seq=n_seq, seq_len=seq_len, n_heads=ret_heads, dk=dk, dv=dv,
                                    cols=(col["rq"], col["rk"], col["rv"], col["rg"]))
        r_s, s_s = retention_sample(proj, state_ret, layer, row0=n_prompt, n_seq=dec_b, chunk=dec_s,
                                    past_len=past_len, n_heads=ret_heads, dk=dk, dv=dv,
                                    cols=(col["rq"], col["rk"], col["rv"], col["rg"]))
        bias = b_sb[layer].astype(F32)
        a_p = sb_prompt(proj, bias, n_seq=n_seq, seq_len=seq_len, n_heads=sb_heads, d=sb_d,
                        cols=(col["sq"], col["sk"], col["sv"]))
        a_s = sb_sample(proj, cache_k, cache_v, page_table, bias, layer, row0=n_prompt, n_seq=dec_b,
                        n_q=dec_s, n_heads=sb_heads, d=sb_d, cols=(col["sq"], col["sk"], col["sv"]))
        pad_rows = n_rows - n_tok
        r = jnp.concatenate([r_p, r_s.astype(BF16), jnp.zeros((pad_rows, d_rv), BF16)], axis=0)
        s = jnp.concatenate([a_p, a_s.astype(BF16), jnp.zeros((pad_rows, d_sb), BF16)], axis=0)
        r_tail = jnp.pad(r_s, ((0, pad_rows), (0, 0)))
        s_tail = jnp.pad(a_s, ((0, pad_rows), (0, 0)))
        u, u_tail = branch_merge(r, s, r_tail, s_tail, w_ret_br, w_sb_br, proj, layer,
                                 ar_col=col["ar"], as_col=col["as"], tm=mm_tm, tn=merge_tn)
        m = matmul(u, u_tail, w_o, layer, tm=mm_tm, tn=out_tn)

        kv_p = lambda c0: proj[:n_prompt, c0:c0 + d_sb].reshape(n_seq, seq_len, sb_heads, sb_d)
        kv_s = lambda c0: proj[n_prompt:n_tok, c0:c0 + d_sb].reshape(dec_b, dec_s, sb_heads, sb_d)
        outs["kp"].append(kv_p(col["sk"]))
        outs["vp"].append(kv_p(col["sv"]))
        outs["ks"].append(kv_s(col["sk"]))
        outs["vs"].append(kv_s(col["sv"]))
        outs["sp"].append(s_p)
        outs["ss"].append(s_s)

        idx = layer // 2
        x, h = norm_mod(x, norm_ffn, layer, mods, resid=(m, GT_M, SC_F, SH_F), out_dtype=F32, **norm_kw)
        if layer % 2 == 0:
            n_tiles = n_rows // dense_tm
            y = fused_mlp(h, w_dense_gate, w_dense_up, w_dense_down,
                          jnp.arange(n_rows, dtype=jnp.int32), jnp.full((n_tiles,), idx, jnp.int32),
                          jnp.ones((n_tiles,), jnp.int32), tm=dense_tm, tf=mlp_tf, tail_row0=n_prompt)
        else:
            route = router(h, w_router, b_router, idx)
            n_tiles = (TOP_K * n_tok) // moe_tm + N_EXPERTS
            tok_of_row, tile_expert, tile_active, pos, wts = _moe_plan(route, n_tok, n_rows, moe_tm, n_tiles)
            n_exp = w_moe_gate.shape[1]
            yg = fused_mlp(h, w_moe_gate.reshape(-1, d_model, ff), w_moe_up.reshape(-1, d_model, ff),
                           w_moe_down.reshape(-1, ff, d_model), tok_of_row, tile_expert + idx * n_exp,
                           tile_active, tm=moe_tm, tf=mlp_tf)
            y = combine(yg, pos[:, 0], pos[:, 1], wts[:, 0], wts[:, 1])
        pending = (y, GT_F, mods)

    m_prev, gt_prev, mods_prev = pending
    y_all = norm_mod(x, norm_final, 0, mods_prev, resid=(m_prev, gt_prev, 0, 0), modulate=False,
                     write_x=False, out_dtype=F32, **norm_kw)
    y_prompt = y_all[:n_prompt].reshape(n_seq, seq_len, d_model)
    y_sample = y_all[n_prompt:n_tok].reshape(dec_b, dec_s, d_model)
    st = lambda k: jnp.stack(outs[k])
    return (y_prompt, y_sample, st("kp"), st("vp"), st("sp"), st("ks"), st("vs"), st("ss"))


def _resid_then_norm(x, m_prev, gt_prev, mods_prev, g, layer, mods, sc_idx, sh_idx, norm_kw):
    mod_seq = jnp.concatenate([mods_prev[0][:, gt_prev:gt_prev + 1], mods[0]], axis=1)
    mod_rows = jnp.concatenate([mods_prev[1][gt_prev:gt_prev + 1], mods[1]], axis=0)
    return norm_mod(x, g, layer, (mod_seq, mod_rows), resid=(m_prev, 0, sc_idx + 1, sh_idx + 1),
                    tail_f32=True, **norm_kw)
```

```python
import functools
import math

import jax
import jax.numpy as jnp
from jax import lax
from jax.experimental import pallas as pl
from jax.experimental.pallas import tpu as pltpu

F32 = jnp.float32
BF16 = jnp.bfloat16

NORM_EPS = 1e-6
ROPE_BASE = 10000.0
N_EXPERTS = 8
TOP_K = 2
RET_CHUNK = 128
ROW_BLOCK = 128
SB_QUERY_BLOCK = 512
SB_KEY_BLOCK = 256
SB_HEAD_GROUP = 2
SB_PAGES_PER_STEP = 4
V7X_VMEM_LIMIT = 56 * 1024 * 1024
V7X_VMEM_LIMIT_LARGE = 62 * 1024 * 1024


def _div_tile(n, cap, mult):
    best = None
    for t in range(mult, min(n, cap) + 1, mult):
        if n % t == 0:
            best = t
    assert best is not None, (n, cap, mult)
    return best


def _cparams(semantics, vmem=V7X_VMEM_LIMIT):
    return pltpu.CompilerParams(dimension_semantics=semantics, vmem_limit_bytes=vmem)


def _nt_dot(a, b):
    return lax.dot_general(a, b, (((1,), (1,)), ((), ())), preferred_element_type=F32)


def _tn_dot(a, b):
    return lax.dot_general(a, b, (((0,), (0,)), ((), ())), preferred_element_type=F32)


def _dot(a, b):
    return jnp.dot(a, b, preferred_element_type=F32)


def _split(x):
    hi = x.astype(BF16)
    return hi, (x - hi.astype(F32)).astype(BF16)


def _dot_hp(a, b, dot=_dot):
    a_hi, a_lo = _split(a)
    b_hi, b_lo = _split(b)
    return dot(a_hi, b_hi) + (dot(a_lo, b_hi) + dot(a_hi, b_lo))


def _adaln_body(c_ref, w_ref, b_ref, o_ref):
    c = c_ref[...]
    o_ref[...] = _dot_hp(c * jax.nn.sigmoid(c), w_ref[...]) + b_ref[...]


def adaln(c_all, w_ada, b_ada, layer, tn=512):
    rows, d = c_all.shape
    n = w_ada.shape[-1]
    b3 = b_ada.reshape(b_ada.shape[0], 1, n)
    return pl.pallas_call(
        _adaln_body,
        grid=(n // tn,),
        in_specs=[pl.BlockSpec((rows, d), lambda j: (0, 0)),
                  pl.BlockSpec((None, d, tn), lambda j: (layer, 0, j)),
                  pl.BlockSpec((None, 1, tn), lambda j: (layer, 0, j))],
        out_specs=pl.BlockSpec((rows, tn), lambda j: (0, j)),
        out_shape=jax.ShapeDtypeStruct((rows, n), F32),
        compiler_params=_cparams(("arbitrary",)),
        name="adaln",
    )(c_all, w_ada, b3)


def _norm_body(*refs, n_seq_blocks, has_resid, write_x, modulate, tail_f32):
    it = iter(refs)
    x_ref = next(it)
    if has_resid:
        m_ref, gts_ref, gtr_ref = next(it), next(it), next(it)
    g_ref = next(it)
    if modulate:
        scs_ref, shs_ref, scr_ref, shr_ref = next(it), next(it), next(it), next(it)
    if write_x:
        xo_ref = next(it)
    h_ref = next(it)
    if tail_f32:
        tail_ref = next(it)
    i = pl.program_id(0)

    def run(gt, sc, sh, is_tail):
        x = x_ref[...]
        if has_resid:
            x = x + gt * m_ref[...]
            if write_x:
                xo_ref[...] = x
        y = x * lax.rsqrt(jnp.mean(x * x, axis=-1, keepdims=True) + NORM_EPS) * g_ref[...]
        if modulate:
            y = y * (1.0 + sc) + sh
        h_ref[...] = y.astype(h_ref.dtype)
        if tail_f32 and is_tail:
            tail_ref[...] = y

    @pl.when(i < n_seq_blocks)
    def _():
        run(gts_ref[...] if has_resid else None,
            scs_ref[...] if modulate else None, shs_ref[...] if modulate else None, False)

    @pl.when(i >= n_seq_blocks)
    def _():
        run(gtr_ref[...] if has_resid else None,
            scr_ref[...] if modulate else None, shr_ref[...] if modulate else None, True)


def norm_mod(x, g, layer, mods, *, seq_len, n_seq, resid=None, modulate=True, write_x=True,
             out_dtype=BF16, tail_f32=False):
    rows, d = x.shape
    tr = ROW_BLOCK
    blocks_per_seq = seq_len // tr
    n_seq_blocks = n_seq * blocks_per_seq
    n_blocks = rows // tr
    mod_seq, mod_rows = mods
    row_spec = pl.BlockSpec((tr, d), lambda i: (i, 0))

    def seq_spec(k):
        return pl.BlockSpec((None, None, 1, d),
                            lambda i: (jnp.minimum(i // blocks_per_seq, n_seq - 1), k, 0, 0))

    def rows_spec(k):
        return pl.BlockSpec((None, tr, d), lambda i: (k, 0, 0))

    args, specs = [x], [row_spec]
    has_resid = resid is not None
    if has_resid:
        m, gt_idx, sc_idx, sh_idx = resid[0], resid[1], resid[2], resid[3]
        args += [m, mod_seq, mod_rows]
        specs += [row_spec, seq_spec(gt_idx), rows_spec(gt_idx)]
    else:
        sc_idx, sh_idx = 1, 0
    args.append(g.reshape(g.shape[0], 1, d) if g.ndim == 2 else g.reshape(1, 1, d))
    specs.append(pl.BlockSpec((None, 1, d), lambda i: (layer if g.ndim == 2 else 0, 0, 0)))
    if modulate:
        args += [mod_seq, mod_seq, mod_rows, mod_rows]
        specs += [seq_spec(sc_idx), seq_spec(sh_idx), rows_spec(sc_idx), rows_spec(sh_idx)]
    out_shape, out_specs = [], []
    if has_resid and write_x:
        out_shape.append(jax.ShapeDtypeStruct((rows, d), F32))
        out_specs.append(row_spec)
    out_shape.append(jax.ShapeDtypeStruct((rows, d), out_dtype))
    out_specs.append(row_spec)
    if tail_f32:
        assert n_blocks == n_seq_blocks + 1
        out_shape.append(jax.ShapeDtypeStruct((tr, d), F32))
        out_specs.append(pl.BlockSpec((tr, d), lambda i: (0, 0)))
    res = pl.pallas_call(
        functools.partial(_norm_body, n_seq_blocks=n_seq_blocks, has_resid=has_resid,
                          write_x=has_resid and write_x, modulate=modulate, tail_f32=tail_f32),
        grid=(n_blocks,),
        in_specs=specs,
        out_specs=out_specs,
        out_shape=out_shape,
        compiler_params=_cparams(("arbitrary",)),
        name="norm_mod",
    )(*args)
    return res if len(res) > 1 else res[0]


def _tail_dot(t_ref, whi_ref, wlo_ref):
    t_hi, t_lo = _split(t_ref[...])
    return _dot(t_hi, whi_ref[...]) + (_dot(t_lo, whi_ref[...]) + _dot(t_hi, wlo_ref[...]))


def _mm_body(x_ref, t_ref, w_ref, o_ref, whi_ref, wlo_ref):
    i = pl.program_id(1)

    @pl.when(i == 0)
    def _():
        hi, lo = _split(w_ref[...])
        whi_ref[...] = hi
        wlo_ref[...] = lo

    o_ref[...] = _dot(x_ref[...], whi_ref[...])

    @pl.when(i == pl.num_programs(1) - 1)
    def _():
        tr = t_ref.shape[0]
        o_ref[o_ref.shape[0] - tr:, :] = _tail_dot(t_ref, whi_ref, wlo_ref)


def matmul(x, tail, w, layer, *, tm, tn):
    m, k = x.shape
    n = w.shape[-1]
    tr = tail.shape[0]
    return pl.pallas_call(
        _mm_body,
        grid=(n // tn, m // tm),
        in_specs=[pl.BlockSpec((tm, k), lambda j, i: (i, 0)),
                  pl.BlockSpec((tr, k), lambda j, i: (0, 0)),
                  pl.BlockSpec((None, k, tn), lambda j, i: (layer, 0, j))],
        out_specs=pl.BlockSpec((tm, tn), lambda j, i: (i, j)),
        out_shape=jax.ShapeDtypeStruct((m, n), F32),
        scratch_shapes=[pltpu.VMEM((k, tn), BF16), pltpu.VMEM((k, tn), BF16)],
        compiler_params=_cparams(("arbitrary", "arbitrary")),
        name="matmul",
    )(x, tail, w)


def _merge_body(r_ref, s_ref, rt_ref, st_ref, wr_ref, ws_ref, ar_ref, as_ref, o_ref, ot_ref,
                wrhi_ref, wrlo_ref, wshi_ref, wslo_ref):
    i = pl.program_id(1)

    @pl.when(i == 0)
    def _():
        wrhi_ref[...], wrlo_ref[...] = _split(wr_ref[...])
        wshi_ref[...], wslo_ref[...] = _split(ws_ref[...])

    gate_r = jax.nn.sigmoid(ar_ref[...])
    gate_s = jax.nn.sigmoid(as_ref[...])
    u = gate_r * _dot(r_ref[...], wrhi_ref[...]) + gate_s * _dot(s_ref[...], wshi_ref[...])
    o_ref[...] = u.astype(o_ref.dtype)

    @pl.when(i == pl.num_programs(1) - 1)
    def _():
        tr = rt_ref.shape[0]
        lo = o_ref.shape[0] - tr
        ut = (gate_r[lo:, :] * _tail_dot(rt_ref, wrhi_ref, wrlo_ref)
              + gate_s[lo:, :] * _tail_dot(st_ref, wshi_ref, wslo_ref))
        ot_ref[...] = ut
        o_ref[lo:, :] = ut.astype(o_ref.dtype)


def branch_merge(r, s, r_tail, s_tail, w_ret_br, w_sb_br, proj, layer, *, ar_col, as_col, tm, tn):
    m, kr = r.shape
    ks = s.shape[1]
    n = w_ret_br.shape[-1]
    tr = r_tail.shape[0]
    return pl.pallas_call(
        _merge_body,
        grid=(n // tn, m // tm),
        in_specs=[pl.BlockSpec((tm, kr), lambda j, i: (i, 0)),
                  pl.BlockSpec((tm, ks), lambda j, i: (i, 0)),
                  pl.BlockSpec((tr, kr), lambda j, i: (0, 0)),
                  pl.BlockSpec((tr, ks), lambda j, i: (0, 0)),
                  pl.BlockSpec((None, kr, tn), lambda j, i: (layer, 0, j)),
                  pl.BlockSpec((None, ks, tn), lambda j, i: (layer, 0, j)),
                  pl.BlockSpec((tm, tn), lambda j, i: (i, ar_col // tn + j)),
                  pl.BlockSpec((tm, tn), lambda j, i: (i, as_col // tn + j))],
        out_specs=[pl.BlockSpec((tm, tn), lambda j, i: (i, j)),
                   pl.BlockSpec((tr, tn), lambda j, i: (0, j))],
        out_shape=[jax.ShapeDtypeStruct((m, n), BF16), jax.ShapeDtypeStruct((tr, n), F32)],
        scratch_shapes=[pltpu.VMEM((kr, tn), BF16), pltpu.VMEM((kr, tn), BF16),
                        pltpu.VMEM((ks, tn), BF16), pltpu.VMEM((ks, tn), BF16)],
        compiler_params=_cparams(("arbitrary", "arbitrary")),
        name="branch_merge",
    )(r, s, r_tail, s_tail, w_ret_br, w_sb_br, proj, proj)


def _ret_tables(n_heads, chunk):
    log_g = jnp.log1p(-jnp.exp2(-5.0 - jnp.arange(n_heads, dtype=F32)))
    idx = jnp.arange(chunk, dtype=F32)
    diff = idx[:, None] - idx[None, :]
    decay = jnp.where(diff >= 0, jnp.exp(jnp.maximum(diff, 0.0)[None] * log_g[:, None, None]), 0.0)
    q_decay = jnp.exp((idx[None, :] + 1.0) * log_g[:, None])[:, :, None]
    k_decay = jnp.exp((chunk - 1.0 - idx)[None, :] * log_g[:, None])[:, :, None]
    c_decay = jnp.exp(chunk * log_g)[:, None, None]
    return decay, q_decay, k_decay, c_decay


def _rope_tables(pos, half):
    inv_freq = jnp.power(ROPE_BASE, -jnp.linspace(0.0, 1.0, half, dtype=F32))
    ang = pos.astype(F32)[:, None] * inv_freq[None, :]
    return jnp.cos(ang), jnp.sin(ang)


def _ret_chunk(q, k, v, g, state, cos, sin, dec, qd, kd, cd, precise=False):
    half = cos.shape[-1]

    def rot(x):
        x1, x2 = x[:, :half], x[:, half:]
        return jnp.concatenate([x1 * cos - x2 * sin, x2 * cos + x1 * sin], axis=-1)

    dk = q.shape[-1]
    qr = rot(q)
    kr = rot(k) * dk ** -0.5
    if precise:
        scores = _dot_hp(qr, kr, _nt_dot) * dec
        o = _dot_hp(scores, v) + _dot_hp(qr, state) * qd
        new_state = cd * state + _dot_hp(kr * kd, v, _tn_dot)
    else:
        qb, kb, vb = qr.astype(BF16), kr.astype(BF16), v.astype(BF16)
        scores = _nt_dot(qb, kb) * dec
        o = _dot(scores.astype(BF16), vb) + _dot(qb, state.astype(BF16)) * qd
        new_state = cd * state + _tn_dot((kr * kd).astype(BF16), vb)
    o = o * lax.rsqrt(jnp.mean(o * o, axis=-1, keepdims=True) + NORM_EPS)
    return o * (g * jax.nn.sigmoid(g)), new_state


def _ret_prompt_body(q_ref, k_ref, v_ref, g_ref, cos_ref, sin_ref, dec_ref, qd_ref, kd_ref, cd_ref,
                     r_ref, s_ref, state_ref):
    @pl.when(pl.program_id(2) == 0)
    def _():
        state_ref[...] = jnp.zeros_like(state_ref)

    r, new_state = _ret_chunk(q_ref[...], k_ref[...], v_ref[...], g_ref[...], state_ref[...],
                              cos_ref[...], sin_ref[...], dec_ref[...], qd_ref[...], kd_ref[...],
                              cd_ref[...])
    r_ref[...] = r.astype(r_ref.dtype)
    state_ref[...] = new_state
    s_ref[...] = new_state


def retention_prompt(proj, *, n_seq, seq_len, n_heads, dk, dv, cols):
    c = RET_CHUNK
    n_chunks = seq_len // c
    dec, qd, kd, cd = _ret_tables(n_heads, c)
    cos, sin = _rope_tables(jnp.arange(seq_len, dtype=jnp.int32), dk // 2)

    def col_spec(col0, width):
        return pl.BlockSpec((c, width), lambda b, h, t: (b * n_chunks + t, col0 // width + h))

    head3 = lambda shape: pl.BlockSpec((None,) + shape, lambda b, h, t: (h, 0, 0))
    rope_spec = pl.BlockSpec((c, dk // 2), lambda b, h, t: (t, 0))
    return pl.pallas_call(
        _ret_prompt_body,
        grid=(n_seq, n_heads, n_chunks),
        in_specs=[col_spec(cols[0], dk), col_spec(cols[1], dk), col_spec(cols[2], dv),
                  col_spec(cols[3], dv), rope_spec, rope_spec,
                  head3((c, c)), head3((c, 1)), head3((c, 1)), head3((1, 1))],
        out_specs=[pl.BlockSpec((c, dv), lambda b, h, t: (b * n_chunks + t, h)),
                   pl.BlockSpec((None, None, dk, dv), lambda b, h, t: (b, h, 0, 0))],
        out_shape=[jax.ShapeDtypeStruct((n_seq * seq_len, n_heads * dv), BF16),
                   jax.ShapeDtypeStruct((n_seq, n_heads, dk, dv), F32)],
        scratch_shapes=[pltpu.VMEM((dk, dv), F32)],
        compiler_params=_cparams(("arbitrary", "arbitrary", "arbitrary")),
        name="retention_prompt",
    )(proj, proj, proj, proj, cos, sin, dec, qd, kd, cd)


def _ret_sample_body(q_ref, k_ref, v_ref, g_ref, cos_ref, sin_ref, dec_ref, qd_ref, kd_ref, cd_ref,
                     s0_ref, r_ref, s_ref, *, n_seq, chunk):
    for b in range(n_seq):
        rows = slice(b * chunk, (b + 1) * chunk)
        r, new_state = _ret_chunk(q_ref[rows, :], k_ref[rows, :], v_ref[rows, :], g_ref[rows, :],
                                  s0_ref[b], cos_ref[...], sin_ref[...], dec_ref[...], qd_ref[...],
                                  kd_ref[...], cd_ref[...], precise=True)
        r_ref[rows, :] = r
        s_ref[b] = new_state


def retention_sample(proj, state, layer, *, row0, n_seq, chunk, past_len, n_heads, dk, dv, cols):
    rows = n_seq * chunk
    dec, qd, kd, cd = _ret_tables(n_heads, chunk)
    cos, sin = _rope_tables(past_len + jnp.arange(chunk, dtype=jnp.int32), dk // 2)

    def col_spec(col0, width):
        return pl.BlockSpec((rows, width), lambda h: (row0 // rows, col0 // width + h))

    head3 = lambda shape: pl.BlockSpec((None,) + shape, lambda h: (h, 0, 0))
    rope_spec = pl.BlockSpec((chunk, dk // 2), lambda h: (0, 0))
    state_spec = pl.BlockSpec((None, n_seq, None, dk, dv), lambda h: (layer, 0, h, 0, 0))
    return pl.pallas_call(
        functools.partial(_ret_sample_body, n_seq=n_seq, chunk=chunk),
        grid=(n_heads,),
        in_specs=[col_spec(cols[0], dk), col_spec(cols[1], dk), col_spec(cols[2], dv),
                  col_spec(cols[3], dv), rope_spec, rope_spec,
                  head3((chunk, chunk)), head3((chunk, 1)), head3((chunk, 1)), head3((1, 1)),
                  state_spec],
        out_specs=[pl.BlockSpec((rows, dv), lambda h: (0, h)),
                   pl.BlockSpec((n_seq, None, dk, dv), lambda h: (0, h, 0, 0))],
        out_shape=[jax.ShapeDtypeStruct((rows, n_heads * dv), F32),
                   jax.ShapeDtypeStruct((n_seq, n_heads, dk, dv), F32)],
        compiler_params=_cparams(("arbitrary",)),
        name="retention_sample",
    )(proj, proj, proj, proj, cos, sin, dec, qd, kd, cd, state)


def _strict_upper_ones(n):
    j = lax.broadcasted_iota(jnp.int32, (n, n), 0)
    s = lax.broadcasted_iota(jnp.int32, (n, n), 1)
    return (j > s).astype(BF16)


def _sb_block(z, mask, run, tri):
    t = jnp.log1p(jnp.exp(-jnp.abs(z)))
    log_sig = jnp.minimum(z, 0.0) - t
    log_not = -jnp.maximum(z, 0.0) - t
    if mask is not None:
        log_not = jnp.where(mask, log_not, 0.0)
    hi = log_not.astype(BF16)
    lo = (log_not - hi.astype(F32)).astype(BF16)
    excl = _dot(hi, tri) + _dot(lo, tri) + run
    w = jnp.exp(log_sig + excl)
    if mask is not None:
        w = jnp.where(mask, w, 0.0)
    return w, run + jnp.sum(log_not, axis=-1, keepdims=True)


def _sb_prompt_body(bias_ref, q_ref, k_ref, v_ref, tri_ref, o_ref, *, scale, d):
    group = pl.program_id(1)
    i = pl.program_id(2)
    tq = q_ref.shape[0]
    tk = tri_ref.shape[0]
    n_group = q_ref.shape[1] // d
    tri = tri_ref[...]
    qs = [q_ref[:, j * d:(j + 1) * d].astype(BF16) for j in range(n_group)]
    biases = [bias_ref[group * n_group + j] for j in range(n_group)]
    qpos = i * tq + lax.broadcasted_iota(jnp.int32, (tq, tk), 0)
    n_diag = tq // tk
    n_key_blocks = (i + 1) * n_diag

    def step(jj, carry, masked):
        k0 = pl.multiple_of((n_key_blocks - 1 - jj) * tk, tk)
        mask = (k0 + lax.broadcasted_iota(jnp.int32, (tq, tk), 1)) < qpos if masked else None
        out = []
        for j in range(n_group):
            run, acc = carry[j]
            k = k_ref[pl.ds(k0, tk), j * d:(j + 1) * d].astype(BF16)
            v = v_ref[pl.ds(k0, tk), j * d:(j + 1) * d].astype(BF16)
            z = _nt_dot(qs[j], k) * scale + biases[j]
            w, run = _sb_block(z, mask, run, tri)
            out.append((run, acc + _dot(w.astype(BF16), v)))
        return tuple(out)

    carry = tuple((jnp.zeros((tq, 1), F32), jnp.zeros((tq, d), F32)) for _ in range(n_group))
    for jj in range(n_diag):
        carry = step(jj, carry, True)
    carry = lax.fori_loop(n_diag, n_key_blocks, lambda jj, c: step(jj, c, False), carry)
    for j in range(n_group):
        o_ref[:, j * d:(j + 1) * d] = carry[j][1].astype(o_ref.dtype)


def sb_prompt(proj, bias, *, n_seq, seq_len, n_heads, d, cols):
    tq = _div_tile(seq_len, SB_QUERY_BLOCK, SB_KEY_BLOCK)
    nq = seq_len // tq
    n_group = SB_HEAD_GROUP if n_heads % SB_HEAD_GROUP == 0 else 1
    gw = n_group * d
    tri = _strict_upper_ones(SB_KEY_BLOCK)
    kv_spec = lambda col0: pl.BlockSpec((seq_len, gw), lambda b, g, i, bias: (b, col0 // gw + g))
    grid_spec = pltpu.PrefetchScalarGridSpec(
        num_scalar_prefetch=1,
        grid=(n_seq, n_heads // n_group, nq),
        in_specs=[pl.BlockSpec((tq, gw), lambda b, g, i, bias: (b * nq + i, cols[0] // gw + g)),
                  kv_spec(cols[1]), kv_spec(cols[2]),
                  pl.BlockSpec((SB_KEY_BLOCK, SB_KEY_BLOCK), lambda b, g, i, bias: (0, 0))],
        out_specs=pl.BlockSpec((tq, gw), lambda b, g, i, bias: (b * nq + i, g)),
    )
    return pl.pallas_call(
        functools.partial(_sb_prompt_body, scale=d ** -0.5, d=d),
        grid_spec=grid_spec,
        out_shape=jax.ShapeDtypeStruct((n_seq * seq_len, n_heads * d), BF16),
        compiler_params=_cparams(("arbitrary", "arbitrary", "arbitrary")),
        name="sb_prompt",
    )(bias, proj, proj, proj, tri)


def _stack_hi_lo(x):
    hi = x.astype(BF16).astype(F32)
    return jnp.concatenate([hi, x - hi], axis=0).astype(BF16)


def _stacked_dot(a2, b, dot):
    b_hi, b_lo = _split(b)
    n = a2.shape[0] // 2
    p, r = dot(a2, b_hi), dot(a2, b_lo)
    return p[:n] + (p[n:] + r[:n])


def _sb_sample_body(pt_ref, q_ref, kn_ref, vn_ref, bias_ref, tri_ref, *rest,
                    scale, n_heads, n_q, d, pages_per_step):
    k_refs = rest[:pages_per_step]
    v_refs = rest[pages_per_step:2 * pages_per_step]
    o_ref, q2_ref, run_ref, acc_ref = rest[2 * pages_per_step:]
    step = pl.program_id(1)
    rows = n_heads * n_q
    page = tri_ref.shape[0]
    tri = tri_ref[...]
    bias = bias_ref[...]

    def attend(k_of_head, v_of_head, mask):
        z = jnp.concatenate([_stacked_dot(q2_ref[h], k_of_head(h), _nt_dot) for h in range(n_heads)],
                            axis=0) * scale + bias
        w, run = _sb_block(z, mask, run_ref[...], tri)
        run_ref[...] = run
        for h in range(n_heads):
            head_rows = slice(h * n_q, (h + 1) * n_q)
            acc_ref[head_rows, :] += _stacked_dot(_stack_hi_lo(w[head_rows, :]), v_of_head(h), _dot)

    @pl.when(step == 0)
    def _():
        for h in range(n_heads):
            q2_ref[h] = _stack_hi_lo(q_ref[:, h * d:(h + 1) * d])
        run_ref[...] = jnp.zeros_like(run_ref)
        acc_ref[...] = jnp.zeros_like(acc_ref)
        pad = jnp.zeros((page - n_q, d), F32)
        q_idx = jnp.concatenate([lax.broadcasted_iota(jnp.int32, (n_q, page), 0)] * n_heads, axis=0)
        key_idx = lax.broadcasted_iota(jnp.int32, (rows, page), 1)
        attend(lambda h: jnp.concatenate([kn_ref[:, h * d:(h + 1) * d], pad], axis=0),
               lambda h: jnp.concatenate([vn_ref[:, h * d:(h + 1) * d], pad], axis=0),
               key_idx < q_idx)

    for u in range(pages_per_step):
        attend(lambda h, ref=k_refs[u]: ref[pl.ds(h, page, stride=n_heads), :],
               lambda h, ref=v_refs[u]: ref[pl.ds(h, page, stride=n_heads), :], None)

    @pl.when(step == pl.num_programs(1) - 1)
    def _():
        for h in range(n_heads):
            o_ref[:, h * d:(h + 1) * d] = acc_ref[h * n_q:(h + 1) * n_q, :]


def sb_sample(proj, cache_k, cache_v, page_table, bias, layer, *, row0, n_seq, n_q, n_heads, d, cols):
    depth, n_pool, page = cache_k.shape[0], cache_k.shape[1], cache_k.shape[2]
    n_pages = page_table.shape[1]
    width = n_heads * d
    ck = cache_k.reshape(depth, n_pool, page * n_heads, d)
    cv = cache_v.reshape(depth, n_pool, page * n_heads, d)
    pps = SB_PAGES_PER_STEP
    n_steps = n_pages // pps
    rows = n_heads * n_q
    bias_col = jnp.repeat(bias.astype(F32), n_q)[:, None]
    tri = _strict_upper_ones(page)

    def new_spec(col0):
        return pl.BlockSpec((n_q, width), lambda b, s, pt: (row0 // n_q + b, col0 // width))

    def page_spec(u):
        return pl.BlockSpec((None, None, page * n_heads, d),
                            lambda b, s, pt: (layer, pt[b, n_pages - 1 - (s * pps + u)], 0, 0))

    grid_spec = pltpu.PrefetchScalarGridSpec(
        num_scalar_prefetch=1,
        grid=(n_seq, n_steps),
        in_specs=[new_spec(cols[0]), new_spec(cols[1]), new_spec(cols[2]),
                  pl.BlockSpec((rows, 1), lambda b, s, pt: (0, 0)),
                  pl.BlockSpec((page, page), lambda b, s, pt: (0, 0))]
                 + [page_spec(u) for u in range(pps)] * 2,
        out_specs=pl.BlockSpec((n_q, width), lambda b, s, pt: (b, 0)),
        scratch_shapes=[pltpu.VMEM((n_heads, 2 * n_q, d), BF16), pltpu.VMEM((rows, 1), F32),
                        pltpu.VMEM((rows, d), F32)],
    )
    return pl.pallas_call(
        functools.partial(_sb_sample_body, scale=d ** -0.5, n_heads=n_heads, n_q=n_q, d=d,
                          pages_per_step=pps),
        grid_spec=grid_spec,
        out_shape=jax.ShapeDtypeStruct((n_seq * n_q, width), F32),
        compiler_params=_cparams(("arbitrary", "arbitrary")),
        name="sb_sample",
    )(page_table, proj, proj, proj, bias_col, tri, *([ck] * pps), *([cv] * pps))


def _mlp_body(tok_ref, texp_ref, tact_ref, h_hbm, wg_ref, wu_ref, wd_ref, y_hbm,
              x_ref, land_ref, acc_ref, sem_in, sem_out, *tail_refs, tm, ch, tail_row0):
    del texp_ref
    i = pl.program_id(0)
    f = pl.program_id(1)
    nf = pl.num_programs(1)
    active = tact_ref[i] > 0
    d = acc_ref.shape[1]
    tn = _div_tile(d, 512, 128)

    def row_copy(tok, r):
        return pltpu.make_async_copy(h_hbm.at[pl.ds(tok, 1)], land_ref.at[pl.ds(r, 1)], sem_in)

    @pl.when(jnp.logical_and(active, f == 0))
    def _():
        acc_ref[...] = jnp.zeros_like(acc_ref)
        for c in range(tm // ch):
            base = i * tm + c * ch

            def issue(r, carry):
                row_copy(tok_ref[base + r], r).start()
                return carry

            def wait(r, carry):
                row_copy(0, r).wait()
                return carry

            lax.fori_loop(0, ch, issue, 0)
            lax.fori_loop(0, ch, wait, 0)
            x_ref[c * ch:(c + 1) * ch, :] = land_ref[...].astype(BF16)

    @pl.when(active)
    def _():
        x = x_ref[...]
        g = _dot(x, wg_ref[...].astype(BF16))
        u = _dot(x, wu_ref[...].astype(BF16))
        a = (g * jax.nn.sigmoid(g) * u).astype(BF16)
        for n in range(d // tn):
            cols = slice(n * tn, (n + 1) * tn)
            acc_ref[:, cols] += _dot(a, wd_ref[:, cols].astype(BF16))

    @pl.when(jnp.logical_and(jnp.logical_not(active), f == 0))
    def _():
        acc_ref[...] = jnp.zeros_like(acc_ref)

    if tail_row0 is not None:
        xt_ref, acct_ref, sem_t = tail_refs
        tr = xt_ref.shape[0]
        tail_tile, tail_off = divmod(tail_row0, tm)
        in_tail_tile = i == tail_tile

        @pl.when(jnp.logical_and(in_tail_tile, f == 0))
        def _():
            load = pltpu.make_async_copy(h_hbm.at[pl.ds(tail_row0, tr)], xt_ref, sem_t)
            load.start()
            load.wait()
            acct_ref[...] = jnp.zeros_like(acct_ref)

        @pl.when(in_tail_tile)
        def _():
            xt = xt_ref[...]
            g = _dot_hp(xt, wg_ref[...])
            u = _dot_hp(xt, wu_ref[...])
            a = g * jax.nn.sigmoid(g) * u
            for n in range(d // tn):
                cols = slice(n * tn, (n + 1) * tn)
                acct_ref[:, cols] += _dot_hp(a, wd_ref[:, cols])

        @pl.when(jnp.logical_and(in_tail_tile, f == nf - 1))
        def _():
            acc_ref[tail_off:tail_off + tr, :] = acct_ref[...]

    @pl.when(f == nf - 1)
    def _():
        out = pltpu.make_async_copy(acc_ref, y_hbm.at[pl.ds(i * tm, tm)], sem_out)
        out.start()
        out.wait()


def fused_mlp(h, w_gate, w_up, w_down, tok_of_row, tile_expert, tile_active, *, tm, tf, tail_row0=None):
    d = h.shape[1]
    ff = w_gate.shape[-1]
    n_tiles = tile_expert.shape[0]
    nf = ff // tf
    ch = _div_tile(tm, 128, 16)
    tail_scratch = []
    if tail_row0 is not None:
        assert tail_row0 % tm + ROW_BLOCK <= tm
        tail_scratch = [pltpu.VMEM((ROW_BLOCK, d), F32), pltpu.VMEM((ROW_BLOCK, d), F32),
                        pltpu.SemaphoreType.DMA]

    def f_idx(i, f, tact):
        return jnp.where(tact[i] > 0, f, nf - 1)

    grid_spec = pltpu.PrefetchScalarGridSpec(
        num_scalar_prefetch=3,
        grid=(n_tiles, nf),
        in_specs=[pl.BlockSpec(memory_space=pl.ANY),
                  pl.BlockSpec((None, d, tf), lambda i, f, tok, texp, tact: (texp[i], 0, f_idx(i, f, tact))),
                  pl.BlockSpec((None, d, tf), lambda i, f, tok, texp, tact: (texp[i], 0, f_idx(i, f, tact))),
                  pl.BlockSpec((None, tf, d), lambda i, f, tok, texp, tact: (texp[i], f_idx(i, f, tact), 0))],
        out_specs=pl.BlockSpec(memory_space=pl.ANY),
        scratch_shapes=[pltpu.VMEM((tm, d), BF16), pltpu.VMEM((ch, d), F32), pltpu.VMEM((tm, d), F32),
                        pltpu.SemaphoreType.DMA, pltpu.SemaphoreType.DMA] + tail_scratch,
    )
    return pl.pallas_call(
        functools.partial(_mlp_body, tm=tm, ch=ch, tail_row0=tail_row0),
        grid_spec=grid_spec,
        out_shape=jax.ShapeDtypeStruct((n_tiles * tm, d), F32),
        compiler_params=_cparams(("arbitrary", "arbitrary"), V7X_VMEM_LIMIT_LARGE),
        name="fused_mlp",
    )(tok_of_row, tile_expert, tile_active, h, w_gate, w_up, w_down)


def _router_body(h_ref, w_ref, b_ref, o_ref, *, n_experts):
    logits = _dot_hp(h_ref[...], w_ref[...]) + b_ref[...]
    lane = lax.broadcasted_iota(jnp.int32, logits.shape, 1).astype(F32)
    neg = jnp.float32(-jnp.inf)
    none = jnp.float32(logits.shape[1])
    logits = jnp.where(lane < n_experts, logits, neg)
    v1 = jnp.max(logits, axis=-1, keepdims=True)
    i1 = jnp.min(jnp.where(logits == v1, lane, none), axis=-1, keepdims=True)
    rest = jnp.where(lane == i1, neg, logits)
    v2 = jnp.max(rest, axis=-1, keepdims=True)
    i2 = jnp.min(jnp.where(rest == v2, lane, none), axis=-1, keepdims=True)
    e2 = jnp.exp(v2 - v1)
    denom = 1.0 + e2
    out = jnp.where(lane == 0, i1,
                    jnp.where(lane == 1, i2,
                              jnp.where(lane == 2, 1.0 / denom,
                                        jnp.where(lane == 3, e2 / denom, 0.0))))
    o_ref[...] = out


def router(h, w_router, b_router, layer):
    rows, d = h.shape
    n_experts = w_router.shape[-1]
    lanes = 128
    w_pad = jnp.pad(w_router, ((0, 0), (0, 0), (0, lanes - n_experts)))
    b_pad = jnp.pad(b_router, ((0, 0), (0, lanes - n_experts))).reshape(-1, 1, lanes)
    tr = ROW_BLOCK
    return pl.pallas_call(
        functools.partial(_router_body, n_experts=n_experts),
        grid=(rows // tr,),
        in_specs=[pl.BlockSpec((tr, d), lambda i: (i, 0)),
                  pl.BlockSpec((None, d, lanes), lambda i: (layer, 0, 0)),
                  pl.BlockSpec((None, 1, lanes), lambda i: (layer, 0, 0))],
        out_specs=pl.BlockSpec((tr, lanes), lambda i: (i, 0)),
        out_shape=jax.ShapeDtypeStruct((rows, lanes), F32),
        compiler_params=_cparams(("arbitrary",)),
        name="router",
    )(h, w_pad, b_pad)


def _combine_body(p0_ref, p1_ref, y_hbm, w0_ref, w1_ref, o_ref, buf0_ref, buf1_ref, sem):
    i = pl.program_id(0)
    tt = o_ref.shape[0]

    def row_copy(src_row, buf_ref, r):
        return pltpu.make_async_copy(y_hbm.at[pl.ds(src_row, 1)], buf_ref.at[pl.ds(r, 1)], sem)

    def issue(r, carry):
        row_copy(p0_ref[i * tt + r], buf0_ref, r).start()
        row_copy(p1_ref[i * tt + r], buf1_ref, r).start()
        return carry

    def wait(r, carry):
        row_copy(0, buf0_ref, r).wait()
        row_copy(0, buf1_ref, r).wait()
        return carry

    lax.fori_loop(0, tt, issue, 0)
    lax.fori_loop(0, tt, wait, 0)
    o_ref[...] = w0_ref[...] * buf0_ref[...] + w1_ref[...] * buf1_ref[...]


def combine(y, p0, p1, w0, w1):
    rows = p0.shape[0]
    d = y.shape[1]
    tt = ROW_BLOCK
    grid_spec = pltpu.PrefetchScalarGridSpec(
        num_scalar_prefetch=2,
        grid=(rows // tt,),
        in_specs=[pl.BlockSpec(memory_space=pl.ANY),
                  pl.BlockSpec((tt, 1), lambda i, p0, p1: (i, 0)),
                  pl.BlockSpec((tt, 1), lambda i, p0, p1: (i, 0))],
        out_specs=pl.BlockSpec((tt, d), lambda i, p0, p1: (i, 0)),
        scratch_shapes=[pltpu.VMEM((tt, d), F32), pltpu.VMEM((tt, d), F32), pltpu.SemaphoreType.DMA],
    )
    return pl.pallas_call(
        _combine_body,
        grid_spec=grid_spec,
        out_shape=jax.ShapeDtypeStruct((rows, d), F32),
        compiler_params=_cparams(("arbitrary",)),
        name="combine",
    )(p0, p1, y, w0[:, None], w1[:, None])


def _moe_plan(route, n_tokens, n_rows, tm, n_tiles):
    e = route[:n_tokens, :TOP_K].astype(jnp.int32)
    w = route[:n_tokens, TOP_K:2 * TOP_K]
    onehot = (e[:, :, None] == jnp.arange(N_EXPERTS)[None, None, :]).astype(jnp.int32)
    per_tok = onehot.sum(axis=1)
    before = jnp.cumsum(per_tok, axis=0) - per_tok
    counts = per_tok.sum(axis=0)
    tiles_per = (counts + tm - 1) // tm
    tile_end = jnp.cumsum(tiles_per)
    tile_start = tile_end - tiles_per
    rank = jnp.take_along_axis(before, e, axis=1)
    pos = tile_start[e] * tm + rank
    tok = jnp.arange(n_tokens, dtype=jnp.int32)
    tok_of_row = jnp.zeros((n_tiles * tm,), jnp.int32)
    tok_of_row = tok_of_row.at[pos[:, 0]].set(tok).at[pos[:, 1]].set(tok)
    tile_ids = jnp.arange(n_tiles, dtype=jnp.int32)
    total = tile_end[-1]
    tile_expert = jnp.sum(jnp.minimum(tile_ids, total - 1)[:, None] >= tile_end[None, :], axis=1)
    tile_expert = jnp.minimum(tile_expert, N_EXPERTS - 1).astype(jnp.int32)
    tile_active = (tile_ids < total).astype(jnp.int32)
    pad = n_rows - n_tokens
    pos = jnp.pad(pos, ((0, pad), (0, 0))).astype(jnp.int32)
    w = jnp.pad(w, ((0, pad), (0, 0)))
    return tok_of_row, tile_expert, tile_active, pos, w


def kernel(x_prompt, x_sample, cache_k, cache_v, state_ret, page_table, c_prompt, c_sample, w_ada, b_ada, norm_mix, norm_ffn, w_in, b_sb, w_ret_br, w_sb_br, w_o, w_dense_gate, w_dense_up, w_dense_down, w_router, b_router, w_moe_gate, w_moe_up, w_moe_down, norm_final):
    n_seq, seq_len, d_model = x_prompt.shape
    dec_b, dec_s, _ = x_sample.shape
    depth = w_in.shape[0]
    ret_heads, dk, dv = state_ret.shape[2], state_ret.shape[3], state_ret.shape[4]
    sb_heads, sb_d = cache_k.shape[3], cache_k.shape[4]
    past_len = page_table.shape[1] * cache_k.shape[2]
    d_rqk, d_rv, d_sb = ret_heads * dk, ret_heads * dv, sb_heads * sb_d
    ff = w_dense_gate.shape[-1]

    n_prompt = n_seq * seq_len
    n_sample = dec_b * dec_s
    n_tok = n_prompt + n_sample
    assert n_sample <= ROW_BLOCK and seq_len % ROW_BLOCK == 0
    n_rows = n_prompt + ROW_BLOCK

    col = {}
    acc = 0
    for name, size in (("rq", d_rqk), ("rk", d_rqk), ("rv", d_rv), ("rg", d_rv),
                       ("sq", d_sb), ("sk", d_sb), ("sv", d_sb), ("ar", d_model), ("as", d_model)):
        col[name] = acc
        acc += size

    mm_tm = _div_tile(n_rows, 768, 16)
    in_tn = _div_tile(w_in.shape[-1], 512, 128)
    out_tn = _div_tile(d_model, 512, 128)
    merge_tn = _div_tile(math.gcd(d_model, col["ar"], col["as"]), 512, 128)
    mlp_tf = _div_tile(ff, 256, 128)
    dense_tm = _div_tile(n_rows, 960, 16)
    moe_tm = -(-(TOP_K * n_tok * 21) // (N_EXPERTS * 20 * 2) // 64) * 64

    x = jnp.concatenate([x_prompt.reshape(n_prompt, d_model), x_sample.reshape(n_sample, d_model),
                         jnp.zeros((n_rows - n_tok, d_model), F32)], axis=0)
    c_all = jnp.concatenate([c_prompt, c_sample,
                             jnp.zeros((-(n_seq + dec_b) % 8, d_model), F32)], axis=0)

    def mods_of(layer):
        mod = adaln(c_all, w_ada, b_ada, layer)
        mod_seq = mod.reshape(mod.shape[0], 6, 1, d_model)
        rows_mod = jnp.repeat(mod[n_seq:n_seq + dec_b].reshape(dec_b, 6, d_model), dec_s, axis=0)
        rows_mod = jnp.pad(rows_mod, ((0, ROW_BLOCK - n_sample), (0, 0), (0, 0)))
        return mod_seq, rows_mod.transpose(1, 0, 2)

    norm_kw = dict(seq_len=seq_len, n_seq=n_seq)
    SH_M, SC_M, GT_M, SH_F, SC_F, GT_F = range(6)

    outs = {k: [] for k in ("kp", "vp", "sp", "ks", "vs", "ss")}
    pending = None
    for layer in range(depth):
        mods = mods_of(layer)
        if pending is None:
            h, h_tail = norm_mod(x, norm_mix, layer, mods, tail_f32=True, **norm_kw)
        else:
            m_prev, gt_prev, mods_prev = pending
            x, h, h_tail = _resid_then_norm(x, m_prev, gt_prev, mods_prev, norm_mix, layer, mods,
                                            SC_M, SH_M, norm_kw)
        proj = matmul(h, h_tail, w_in, layer, tm=mm_tm, tn=in_tn)

        r_p, s_p = retention_prompt(proj, n_seq=n_seq, seq_len=seq_len, n_heads=ret_heads, dk=dk, dv=dv,
                                    cols=(col["rq"], col["rk"], col["rv"], col["rg"]))
        r_s, s_s = retention_sample(proj, state_ret, layer, row0=n_prompt, n_seq=dec_b, chunk=dec_s,
                                    past_len=past_len, n_heads=ret_heads, dk=dk, dv=dv,
                                    cols=(col["rq"], col["rk"], col["rv"], col["rg"]))
        bias = b_sb[layer].astype(F32)
        a_p = sb_prompt(proj, bias, n_seq=n_seq, seq_len=seq_len, n_heads=sb_heads, d=sb_d,
                        cols=(col["sq"], col["sk"], col["sv"]))
        a_s = sb_sample(proj, cache_k, cache_v, page_table, bias, layer, row0=n_prompt, n_seq=dec_b,
                        n_q=dec_s, n_heads=sb_heads, d=sb_d, cols=(col["sq"], col["sk"], col["sv"]))
        pad_rows = n_rows - n_tok
        r = jnp.concatenate([r_p, r_s.astype(BF16), jnp.zeros((pad_rows, d_rv), BF16)], axis=0)
        s = jnp.concatenate([a_p, a_s.astype(BF16), jnp.zeros((pad_rows, d_sb), BF16)], axis=0)
        r_tail = jnp.pad(r_s, ((0, pad_rows), (0, 0)))
        s_tail = jnp.pad(a_s, ((0, pad_rows), (0, 0)))
        u, u_tail = branch_merge(r, s, r_tail, s_tail, w_ret_br, w_sb_br, proj, layer,
                                 ar_col=col["ar"], as_col=col["as"], tm=mm_tm, tn=merge_tn)
        m = matmul(u, u_tail, w_o, layer, tm=mm_tm, tn=out_tn)

        kv_p = lambda c0: proj[:n_prompt, c0:c0 + d_sb].reshape(n_seq, seq_len, sb_heads, sb_d)
        kv_s = lambda c0: proj[n_prompt:n_tok, c0:c0 + d_sb].reshape(dec_b, dec_s, sb_heads, sb_d)
        outs["kp"].append(kv_p(col["sk"]))
        outs["vp"].append(kv_p(col["sv"]))
        outs["ks"].append(kv_s(col["sk"]))
        outs["vs"].append(kv_s(col["sv"]))
        outs["sp"].append(s_p)
        outs["ss"].append(s_s)

        idx = layer // 2
        x, h = norm_mod(x, norm_ffn, layer, mods, resid=(m, GT_M, SC_F, SH_F), out_dtype=F32, **norm_kw)
        if layer % 2 == 0:
            n_tiles = n_rows // dense_tm
            y = fused_mlp(h, w_dense_gate, w_dense_up, w_dense_down,
                          jnp.arange(n_rows, dtype=jnp.int32), jnp.full((n_tiles,), idx, jnp.int32),
                          jnp.ones((n_tiles,), jnp.int32), tm=dense_tm, tf=mlp_tf, tail_row0=n_prompt)
        else:
            route = router(h, w_router, b_router, idx)
            n_tiles = (TOP_K * n_tok) // moe_tm + N_EXPERTS
            tok_of_row, tile_expert, tile_active, pos, wts = _moe_plan(route, n_tok, n_rows, moe_tm, n_tiles)
            n_exp = w_moe_gate.shape[1]
            yg = fused_mlp(h, w_moe_gate.reshape(-1, d_model, ff), w_moe_up.reshape(-1, d_model, ff),
                           w_moe_down.reshape(-1, ff, d_model), tok_of_row, tile_expert + idx * n_exp,
                           tile_active, tm=moe_tm, tf=mlp_tf)
            y = combine(yg, pos[:, 0], pos[:, 1], wts[:, 0], wts[:, 1])
        pending = (y, GT_F, mods)

    m_prev, gt_prev, mods_prev = pending
    y_all = norm_mod(x, norm_final, 0, mods_prev, resid=(m_prev, gt_prev, 0, 0), modulate=False,
                     write_x=False, out_dtype=F32, **norm_kw)
    y_prompt = y_all[:n_prompt].reshape(n_seq, seq_len, d_model)
    y_sample = y_all[n_prompt:n_tok].reshape(dec_b, dec_s, d_model)
    st = lambda k: jnp.stack(outs[k])
    return (y_prompt, y_sample, st("kp"), st("vp"), st("sp"), st("ks"), st("vs"), st("ss"))


def _resid_then_norm(x, m_prev, gt_prev, mods_prev, g, layer, mods, sc_idx, sh_idx, norm_kw):
    mod_seq = jnp.concatenate([mods_prev[0][:, gt_prev:gt_prev + 1], mods[0]], axis=1)
    mod_rows = jnp.concatenate([mods_prev[1][gt_prev:gt_prev + 1], mods[1]], axis=0)
    return norm_mod(x, g, layer, (mod_seq, mod_rows), resid=(m_prev, 0, sc_idx + 1, sh_idx + 1),
                    tail_f32=True, **norm_kw)
```

```python
import functools
import math

import jax
import jax.numpy as jnp
from jax import lax
from jax.experimental import pallas as pl
from jax.experimental.pallas import tpu as pltpu

F32 = jnp.float32
BF16 = jnp.bfloat16

NORM_EPS = 1e-6
ROPE_BASE = 10000.0
N_EXPERTS = 8
TOP_K = 2
RET_CHUNK = 128
ROW_BLOCK = 128
SB_QUERY_BLOCK = 512
SB_KEY_BLOCK = 256
SB_HEAD_GROUP = 2
SB_PAGES_PER_STEP = 4
V7X_VMEM_LIMIT = 56 * 1024 * 1024
V7X_VMEM_LIMIT_LARGE = 62 * 1024 * 1024


def _div_tile(n, cap, mult):
    best = None
    for t in range(mult, min(n, cap) + 1, mult):
        if n % t == 0:
            best = t
    assert best is not None, (n, cap, mult)
    return best


def _cparams(semantics, vmem=V7X_VMEM_LIMIT):
    return pltpu.CompilerParams(dimension_semantics=semantics, vmem_limit_bytes=vmem)


def _nt_dot(a, b):
    return lax.dot_general(a, b, (((1,), (1,)), ((), ())), preferred_element_type=F32)


def _tn_dot(a, b):
    return lax.dot_general(a, b, (((0,), (0,)), ((), ())), preferred_element_type=F32)


def _dot(a, b):
    return jnp.dot(a, b, preferred_element_type=F32)


def _split(x):
    hi = x.astype(BF16)
    return hi, (x - hi.astype(F32)).astype(BF16)


def _dot_hp(a, b, dot=_dot):
    a_hi, a_lo = _split(a)
    b_hi, b_lo = _split(b)
    return dot(a_hi, b_hi) + (dot(a_lo, b_hi) + dot(a_hi, b_lo))


def _adaln_body(c_ref, w_ref, b_ref, o_ref):
    c = c_ref[...]
    o_ref[...] = _dot_hp(c * jax.nn.sigmoid(c), w_ref[...]) + b_ref[...]


def adaln(c_all, w_ada, b_ada, layer, tn=512):
    rows, d = c_all.shape
    n = w_ada.shape[-1]
    b3 = b_ada.reshape(b_ada.shape[0], 1, n)
    return pl.pallas_call(
        _adaln_body,
        grid=(n // tn,),
        in_specs=[pl.BlockSpec((rows, d), lambda j: (0, 0)),
                  pl.BlockSpec((None, d, tn), lambda j: (layer, 0, j)),
                  pl.BlockSpec((None, 1, tn), lambda j: (layer, 0, j))],
        out_specs=pl.BlockSpec((rows, tn), lambda j: (0, j)),
        out_shape=jax.ShapeDtypeStruct((rows, n), F32),
        compiler_params=_cparams(("arbitrary",)),
        name="adaln",
    )(c_all, w_ada, b3)


def _norm_body(*refs, n_seq_blocks, has_resid, write_x, modulate, tail_f32):
    it = iter(refs)
    x_ref = next(it)
    if has_resid:
        m_ref, gts_ref, gtr_ref = next(it), next(it), next(it)
    g_ref = next(it)
    if modulate:
        scs_ref, shs_ref, scr_ref, shr_ref = next(it), next(it), next(it), next(it)
    if write_x:
        xo_ref = next(it)
    h_ref = next(it)
    if tail_f32:
        tail_ref = next(it)
    i = pl.program_id(0)

    def run(gt, sc, sh, is_tail):
        x = x_ref[...]
        if has_resid:
            x = x + gt * m_ref[...]
            if write_x:
                xo_ref[...] = x
        y = x * lax.rsqrt(jnp.mean(x * x, axis=-1, keepdims=True) + NORM_EPS) * g_ref[...]
        if modulate:
            y = y * (1.0 + sc) + sh
        h_ref[...] = y.astype(h_ref.dtype)
        if tail_f32 and is_tail:
            tail_ref[...] = y

    @pl.when(i < n_seq_blocks)
    def _():
        run(gts_ref[...] if has_resid else None,
            scs_ref[...] if modulate else None, shs_ref[...] if modulate else None, False)

    @pl.when(i >= n_seq_blocks)
    def _():
        run(gtr_ref[...] if has_resid else None,
            scr_ref[...] if modulate else None, shr_ref[...] if modulate else None, True)


def norm_mod(x, g, layer, mods, *, seq_len, n_seq, resid=None, modulate=True, write_x=True,
             out_dtype=BF16, tail_f32=False):
    rows, d = x.shape
    tr = ROW_BLOCK
    blocks_per_seq = seq_len // tr
    n_seq_blocks = n_seq * blocks_per_seq
    n_blocks = rows // tr
    mod_seq, mod_rows = mods
    row_spec = pl.BlockSpec((tr, d), lambda i: (i, 0))

    def seq_spec(k):
        return pl.BlockSpec((None, None, 1, d),
                            lambda i: (jnp.minimum(i // blocks_per_seq, n_seq - 1), k, 0, 0))

    def rows_spec(k):
        return pl.BlockSpec((None, tr, d), lambda i: (k, 0, 0))

    args, specs = [x], [row_spec]
    has_resid = resid is not None
    if has_resid:
        m, gt_idx, sc_idx, sh_idx = resid[0], resid[1], resid[2], resid[3]
        args += [m, mod_seq, mod_rows]
        specs += [row_spec, seq_spec(gt_idx), rows_spec(gt_idx)]
    else:
        sc_idx, sh_idx = 1, 0
    args.append(g.reshape(g.shape[0], 1, d) if g.ndim == 2 else g.reshape(1, 1, d))
    specs.append(pl.BlockSpec((None, 1, d), lambda i: (layer if g.ndim == 2 else 0, 0, 0)))
    if modulate:
        args += [mod_seq, mod_seq, mod_rows, mod_rows]
        specs += [seq_spec(sc_idx), seq_spec(sh_idx), rows_spec(sc_idx), rows_spec(sh_idx)]
    out_shape, out_specs = [], []
    if has_resid and write_x:
        out_shape.append(jax.ShapeDtypeStruct((rows, d), F32))
        out_specs.append(row_spec)
    out_shape.append(jax.ShapeDtypeStruct((rows, d), out_dtype))
    out_specs.append(row_spec)
    if tail_f32:
        assert n_blocks == n_seq_blocks + 1
        out_shape.append(jax.ShapeDtypeStruct((tr, d), F32))
        out_specs.append(pl.BlockSpec((tr, d), lambda i: (0, 0)))
    res = pl.pallas_call(
        functools.partial(_norm_body, n_seq_blocks=n_seq_blocks, has_resid=has_resid,
                          write_x=has_resid and write_x, modulate=modulate, tail_f32=tail_f32),
        grid=(n_blocks,),
        in_specs=specs,
        out_specs=out_specs,
        out_shape=out_shape,
        compiler_params=_cparams(("arbitrary",)),
        name="norm_mod",
    )(*args)
    return res if len(res) > 1 else res[0]


def _tail_dot(t_ref, whi_ref, wlo_ref):
    t_hi, t_lo = _split(t_ref[...])
    return _dot(t_hi, whi_ref[...]) + (_dot(t_lo, whi_ref[...]) + _dot(t_hi, wlo_ref[...]))


def _mm_body(x_ref, t_ref, w_ref, o_ref, whi_ref, wlo_ref):
    i = pl.program_id(1)

    @pl.when(i == 0)
    def _():
        hi, lo = _split(w_ref[...])
        whi_ref[...] = hi
        wlo_ref[...] = lo

    o_ref[...] = _dot(x_ref[...], whi_ref[...])

    @pl.when(i == pl.num_programs(1) - 1)
    def _():
        tr = t_ref.shape[0]
        o_ref[o_ref.shape[0] - tr:, :] = _tail_dot(t_ref, whi_ref, wlo_ref)


def matmul(x, tail, w, layer, *, tm, tn):
    m, k = x.shape
    n = w.shape[-1]
    tr = tail.shape[0]
    return pl.pallas_call(
        _mm_body,
        grid=(n // tn, m // tm),
        in_specs=[pl.BlockSpec((tm, k), lambda j, i: (i, 0)),
                  pl.BlockSpec((tr, k), lambda j, i: (0, 0)),
                  pl.BlockSpec((None, k, tn), lambda j, i: (layer, 0, j))],
        out_specs=pl.BlockSpec((tm, tn), lambda j, i: (i, j)),
        out_shape=jax.ShapeDtypeStruct((m, n), F32),
        scratch_shapes=[pltpu.VMEM((k, tn), BF16), pltpu.VMEM((k, tn), BF16)],
        compiler_params=_cparams(("arbitrary", "arbitrary")),
        name="matmul",
    )(x, tail, w)


def _merge_body(r_ref, s_ref, rt_ref, st_ref, wr_ref, ws_ref, ar_ref, as_ref, o_ref, ot_ref,
                wrhi_ref, wrlo_ref, wshi_ref, wslo_ref):
    i = pl.program_id(1)

    @pl.when(i == 0)
    def _():
        wrhi_ref[...], wrlo_ref[...] = _split(wr_ref[...])
        wshi_ref[...], wslo_ref[...] = _split(ws_ref[...])

    gate_r = jax.nn.sigmoid(ar_ref[...])
    gate_s = jax.nn.sigmoid(as_ref[...])
    u = gate_r * _dot(r_ref[...], wrhi_ref[...]) + gate_s * _dot(s_ref[...], wshi_ref[...])
    o_ref[...] = u.astype(o_ref.dtype)

    @pl.when(i == pl.num_programs(1) - 1)
    def _():
        tr = rt_ref.shape[0]
        lo = o_ref.shape[0] - tr
        ut = (gate_r[lo:, :] * _tail_dot(rt_ref, wrhi_ref, wrlo_ref)
              + gate_s[lo:, :] * _tail_dot(st_ref, wshi_ref, wslo_ref))
        ot_ref[...] = ut
        o_ref[lo:, :] = ut.astype(o_ref.dtype)


def branch_merge(r, s, r_tail, s_tail, w_ret_br, w_sb_br, proj, layer, *, ar_col, as_col, tm, tn):
    m, kr = r.shape
    ks = s.shape[1]
    n = w_ret_br.shape[-1]
    tr = r_tail.shape[0]
    return pl.pallas_call(
        _merge_body,
        grid=(n // tn, m // tm),
        in_specs=[pl.BlockSpec((tm, kr), lambda j, i: (i, 0)),
                  pl.BlockSpec((tm, ks), lambda j, i: (i, 0)),
                  pl.BlockSpec((tr, kr), lambda j, i: (0, 0)),
                  pl.BlockSpec((tr, ks), lambda j, i: (0, 0)),
                  pl.BlockSpec((None, kr, tn), lambda j, i: (layer, 0, j)),
                  pl.BlockSpec((None, ks, tn), lambda j, i: (layer, 0, j)),
                  pl.BlockSpec((tm, tn), lambda j, i: (i, ar_col // tn + j)),
                  pl.BlockSpec((tm, tn), lambda j, i: (i, as_col // tn + j))],
        out_specs=[pl.BlockSpec((tm, tn), lambda j, i: (i, j)),
                   pl.BlockSpec((tr, tn), lambda j, i: (0, j))],
        out_shape=[jax.ShapeDtypeStruct((m, n), BF16), jax.ShapeDtypeStruct((tr, n), F32)],
        scratch_shapes=[pltpu.VMEM((kr, tn), BF16), pltpu.VMEM((kr, tn), BF16),
                        pltpu.VMEM((ks, tn), BF16), pltpu.VMEM((ks, tn), BF16)],
        compiler_params=_cparams(("arbitrary", "arbitrary")),
        name="branch_merge",
    )(r, s, r_tail, s_tail, w_ret_br, w_sb_br, proj, proj)


def _ret_tables(n_heads, chunk):
    log_g = jnp.log1p(-jnp.exp2(-5.0 - jnp.arange(n_heads, dtype=F32)))
    idx = jnp.arange(chunk, dtype=F32)
    diff = idx[:, None] - idx[None, :]
    decay = jnp.where(diff >= 0, jnp.exp(jnp.maximum(diff, 0.0)[None] * log_g[:, None, None]), 0.0)
    q_decay = jnp.exp((idx[None, :] + 1.0) * log_g[:, None])[:, :, None]
    k_decay = jnp.exp((chunk - 1.0 - idx)[None, :] * log_g[:, None])[:, :, None]
    c_decay = jnp.exp(chunk * log_g)[:, None, None]
    return decay, q_decay, k_decay, c_decay


def _rope_tables(pos, half):
    inv_freq = jnp.power(ROPE_BASE, -jnp.linspace(0.0, 1.0, half, dtype=F32))
    ang = pos.astype(F32)[:, None] * inv_freq[None, :]
    return jnp.cos(ang), jnp.sin(ang)


def _ret_chunk(q, k, v, g, state, cos, sin, dec, qd, kd, cd, precise=False):
    half = cos.shape[-1]

    def rot(x):
        x1, x2 = x[:, :half], x[:, half:]
        return jnp.concatenate([x1 * cos - x2 * sin, x2 * cos + x1 * sin], axis=-1)

    dk = q.shape[-1]
    qr = rot(q)
    kr = rot(k) * dk ** -0.5
    if precise:
        scores = _dot_hp(qr, kr, _nt_dot) * dec
        o = _dot_hp(scores, v) + _dot_hp(qr, state) * qd
        new_state = cd * state + _dot_hp(kr * kd, v, _tn_dot)
    else:
        qb, kb, vb = qr.astype(BF16), kr.astype(BF16), v.astype(BF16)
        scores = _nt_dot(qb, kb) * dec
        o = _dot(scores.astype(BF16), vb) + _dot(qb, state.astype(BF16)) * qd
        new_state = cd * state + _tn_dot((kr * kd).astype(BF16), vb)
    o = o * lax.rsqrt(jnp.mean(o * o, axis=-1, keepdims=True) + NORM_EPS)
    return o * (g * jax.nn.sigmoid(g)), new_state


def _ret_prompt_body(q_ref, k_ref, v_ref, g_ref, cos_ref, sin_ref, dec_ref, qd_ref, kd_ref, cd_ref,
                     r_ref, s_ref, state_ref):
    @pl.when(pl.program_id(2) == 0)
    def _():
        state_ref[...] = jnp.zeros_like(state_ref)

    r, new_state = _ret_chunk(q_ref[...], k_ref[...], v_ref[...], g_ref[...], state_ref[...],
                              cos_ref[...], sin_ref[...], dec_ref[...], qd_ref[...], kd_ref[...],
                              cd_ref[...])
    r_ref[...] = r.astype(r_ref.dtype)
    state_ref[...] = new_state
    s_ref[...] = new_state


def retention_prompt(proj, *, n_seq, seq_len, n_heads, dk, dv, cols):
    c = RET_CHUNK
    n_chunks = seq_len // c
    dec, qd, kd, cd = _ret_tables(n_heads, c)
    cos, sin = _rope_tables(jnp.arange(seq_len, dtype=jnp.int32), dk // 2)

    def col_spec(col0, width):
        return pl.BlockSpec((c, width), lambda b, h, t: (b * n_chunks + t, col0 // width + h))

    head3 = lambda shape: pl.BlockSpec((None,) + shape, lambda b, h, t: (h, 0, 0))
    rope_spec = pl.BlockSpec((c, dk // 2), lambda b, h, t: (t, 0))
    return pl.pallas_call(
        _ret_prompt_body,
        grid=(n_seq, n_heads, n_chunks),
        in_specs=[col_spec(cols[0], dk), col_spec(cols[1], dk), col_spec(cols[2], dv),
                  col_spec(cols[3], dv), rope_spec, rope_spec,
                  head3((c, c)), head3((c, 1)), head3((c, 1)), head3((1, 1))],
        out_specs=[pl.BlockSpec((c, dv), lambda b, h, t: (b * n_chunks + t, h)),
                   pl.BlockSpec((None, None, dk, dv), lambda b, h, t: (b, h, 0, 0))],
        out_shape=[jax.ShapeDtypeStruct((n_seq * seq_len, n_heads * dv), BF16),
                   jax.ShapeDtypeStruct((n_seq, n_heads, dk, dv), F32)],
        scratch_shapes=[pltpu.VMEM((dk, dv), F32)],
        compiler_params=_cparams(("arbitrary", "arbitrary", "arbitrary")),
        name="retention_prompt",
    )(proj, proj, proj, proj, cos, sin, dec, qd, kd, cd)


def _ret_sample_body(q_ref, k_ref, v_ref, g_ref, cos_ref, sin_ref, dec_ref, qd_ref, kd_ref, cd_ref,
                     s0_ref, r_ref, s_ref, *, n_seq, chunk):
    for b in range(n_seq):
        rows = slice(b * chunk, (b + 1) * chunk)
        r, new_state = _ret_chunk(q_ref[rows, :], k_ref[rows, :], v_ref[rows, :], g_ref[rows, :],
                                  s0_ref[b], cos_ref[...], sin_ref[...], dec_ref[...], qd_ref[...],
                                  kd_ref[...], cd_ref[...], precise=True)
        r_ref[rows, :] = r
        s_ref[b] = new_state


def retention_sample(proj, state, layer, *, row0, n_seq, chunk, past_len, n_heads, dk, dv, cols):
    rows = n_seq * chunk
    dec, qd, kd, cd = _ret_tables(n_heads, chunk)
    cos, sin = _rope_tables(past_len + jnp.arange(chunk, dtype=jnp.int32), dk // 2)

    def col_spec(col0, width):
        return pl.BlockSpec((rows, width), lambda h: (row0 // rows, col0 // width + h))

    head3 = lambda shape: pl.BlockSpec((None,) + shape, lambda h: (h, 0, 0))
    rope_spec = pl.BlockSpec((chunk, dk // 2), lambda h: (0, 0))
    state_spec = pl.BlockSpec((None, n_seq, None, dk, dv), lambda h: (layer, 0, h, 0, 0))
    return pl.pallas_call(
        functools.partial(_ret_sample_body, n_seq=n_seq, chunk=chunk),
        grid=(n_heads,),
        in_specs=[col_spec(cols[0], dk), col_spec(cols[1], dk), col_spec(cols[2], dv),
                  col_spec(cols[3], dv), rope_spec, rope_spec,
                  head3((chunk, chunk)), head3((chunk, 1)), head3((chunk, 1)), head3((1, 1)),
                  state_spec],
        out_specs=[pl.BlockSpec((rows, dv), lambda h: (0, h)),
                   pl.BlockSpec((n_seq, None, dk, dv), lambda h: (0, h, 0, 0))],
        out_shape=[jax.ShapeDtypeStruct((rows, n_heads * dv), F32),
                   jax.ShapeDtypeStruct((n_seq, n_heads, dk, dv), F32)],
        compiler_params=_cparams(("arbitrary",)),
        name="retention_sample",
    )(proj, proj, proj, proj, cos, sin, dec, qd, kd, cd, state)


def _strict_upper_ones(n):
    j = lax.broadcasted_iota(jnp.int32, (n, n), 0)
    s = lax.broadcasted_iota(jnp.int32, (n, n), 1)
    return (j > s).astype(BF16)


def _sb_block(z, mask, run, tri):
    t = jnp.log1p(jnp.exp(-jnp.abs(z)))
    log_sig = jnp.minimum(z, 0.0) - t
    log_not = -jnp.maximum(z, 0.0) - t
    if mask is not None:
        log_not = jnp.where(mask, log_not, 0.0)
    hi = log_not.astype(BF16)
    lo = (log_not - hi.astype(F32)).astype(BF16)
    excl = _dot(hi, tri) + _dot(lo, tri) + run
    w = jnp.exp(log_sig + excl)
    if mask is not None:
        w = jnp.where(mask, w, 0.0)
    return w, run + jnp.sum(log_not, axis=-1, keepdims=True)


def _sb_prompt_body(bias_ref, q_ref, k_ref, v_ref, tri_ref, o_ref, *, scale, d):
    group = pl.program_id(1)
    i = pl.program_id(2)
    tq = q_ref.shape[0]
    tk = tri_ref.shape[0]
    n_group = q_ref.shape[1] // d
    tri = tri_ref[...]
    qs = [q_ref[:, j * d:(j + 1) * d].astype(BF16) for j in range(n_group)]
    biases = [bias_ref[group * n_group + j] for j in range(n_group)]
    qpos = i * tq + lax.broadcasted_iota(jnp.int32, (tq, tk), 0)
    n_diag = tq // tk
    n_key_blocks = (i + 1) * n_diag

    def step(jj, carry, masked):
        k0 = pl.multiple_of((n_key_blocks - 1 - jj) * tk, tk)
        mask = (k0 + lax.broadcasted_iota(jnp.int32, (tq, tk), 1)) < qpos if masked else None
        out = []
        for j in range(n_group):
            run, acc = carry[j]
            k = k_ref[pl.ds(k0, tk), j * d:(j + 1) * d].astype(BF16)
            v = v_ref[pl.ds(k0, tk), j * d:(j + 1) * d].astype(BF16)
            z = _nt_dot(qs[j], k) * scale + biases[j]
            w, run = _sb_block(z, mask, run, tri)
            out.append((run, acc + _dot(w.astype(BF16), v)))
        return tuple(out)

    carry = tuple((jnp.zeros((tq, 1), F32), jnp.zeros((tq, d), F32)) for _ in range(n_group))
    for jj in range(n_diag):
        carry = step(jj, carry, True)
    carry = lax.fori_loop(n_diag, n_key_blocks, lambda jj, c: step(jj, c, False), carry)
    for j in range(n_group):
        o_ref[:, j * d:(j + 1) * d] = carry[j][1].astype(o_ref.dtype)


def sb_prompt(proj, bias, *, n_seq, seq_len, n_heads, d, cols):
    tq = _div_tile(seq_len, SB_QUERY_BLOCK, SB_KEY_BLOCK)
    nq = seq_len // tq
    n_group = SB_HEAD_GROUP if n_heads % SB_HEAD_GROUP == 0 else 1
    gw = n_group * d
    tri = _strict_upper_ones(SB_KEY_BLOCK)
    kv_spec = lambda col0: pl.BlockSpec((seq_len, gw), lambda b, g, i, bias: (b, col0 // gw + g))
    grid_spec = pltpu.PrefetchScalarGridSpec(
        num_scalar_prefetch=1,
        grid=(n_seq, n_heads // n_group, nq),
        in_specs=[pl.BlockSpec((tq, gw), lambda b, g, i, bias: (b * nq + i, cols[0] // gw + g)),
                  kv_spec(cols[1]), kv_spec(cols[2]),
                  pl.BlockSpec((SB_KEY_BLOCK, SB_KEY_BLOCK), lambda b, g, i, bias: (0, 0))],
        out_specs=pl.BlockSpec((tq, gw), lambda b, g, i, bias: (b * nq + i, g)),
    )
    return pl.pallas_call(
        functools.partial(_sb_prompt_body, scale=d ** -0.5, d=d),
        grid_spec=grid_spec,
        out_shape=jax.ShapeDtypeStruct((n_seq * seq_len, n_heads * d), BF16),
        compiler_params=_cparams(("arbitrary", "arbitrary", "arbitrary")),
        name="sb_prompt",
    )(bias, proj, proj, proj, tri)


def _stack_hi_lo(x):
    hi = x.astype(BF16).astype(F32)
    return jnp.concatenate([hi, x - hi], axis=0).astype(BF16)


def _stacked_dot(a2, b, dot):
    b_hi, b_lo = _split(b)
    n = a2.shape[0] // 2
    p, r = dot(a2, b_hi), dot(a2, b_lo)
    return p[:n] + (p[n:] + r[:n])


def _sb_sample_body(pt_ref, q_ref, kn_ref, vn_ref, bias_ref, tri_ref, *rest,
                    scale, n_heads, n_q, d, pages_per_step):
    k_refs = rest[:pages_per_step]
    v_refs = rest[pages_per_step:2 * pages_per_step]
    o_ref, q2_ref, run_ref, acc_ref = rest[2 * pages_per_step:]
    step = pl.program_id(1)
    rows = n_heads * n_q
    page = tri_ref.shape[0]
    tri = tri_ref[...]
    bias = bias_ref[...]

    def attend(k_of_head, v_of_head, mask):
        z = jnp.concatenate([_stacked_dot(q2_ref[h], k_of_head(h), _nt_dot) for h in range(n_heads)],
                            axis=0) * scale + bias
        w, run = _sb_block(z, mask, run_ref[...], tri)
        run_ref[...] = run
        for h in range(n_heads):
            head_rows = slice(h * n_q, (h + 1) * n_q)
            acc_ref[head_rows, :] += _stacked_dot(_stack_hi_lo(w[head_rows, :]), v_of_head(h), _dot)

    @pl.when(step == 0)
    def _():
        for h in range(n_heads):
            q2_ref[h] = _stack_hi_lo(q_ref[:, h * d:(h + 1) * d])
        run_ref[...] = jnp.zeros_like(run_ref)
        acc_ref[...] = jnp.zeros_like(acc_ref)
        pad = jnp.zeros((page - n_q, d), F32)
        q_idx = jnp.concatenate([lax.broadcasted_iota(jnp.int32, (n_q, page), 0)] * n_heads, axis=0)
        key_idx = lax.broadcasted_iota(jnp.int32, (rows, page), 1)
        attend(lambda h: jnp.concatenate([kn_ref[:, h * d:(h + 1) * d], pad], axis=0),
               lambda h: jnp.concatenate([vn_ref[:, h * d:(h + 1) * d], pad], axis=0),
               key_idx < q_idx)

    for u in range(pages_per_step):
        attend(lambda h, ref=k_refs[u]: ref[pl.ds(h, page, stride=n_heads), :],
               lambda h, ref=v_refs[u]: ref[pl.ds(h, page, stride=n_heads), :], None)

    @pl.when(step == pl.num_programs(1) - 1)
    def _():
        for h in range(n_heads):
            o_ref[:, h * d:(h + 1) * d] = acc_ref[h * n_q:(h + 1) * n_q, :]


def sb_sample(proj, cache_k, cache_v, page_table, bias, layer, *, row0, n_seq, n_q, n_heads, d, cols):
    depth, n_pool, page = cache_k.shape[0], cache_k.shape[1], cache_k.shape[2]
    n_pages = page_table.shape[1]
    width = n_heads * d
    ck = cache_k.reshape(depth, n_pool, page * n_heads, d)
    cv = cache_v.reshape(depth, n_pool, page * n_heads, d)
    pps = SB_PAGES_PER_STEP
    n_steps = n_pages // pps
    rows = n_heads * n_q
    bias_col = jnp.repeat(bias.astype(F32), n_q)[:, None]
    tri = _strict_upper_ones(page)

    def new_spec(col0):
        return pl.BlockSpec((n_q, width), lambda b, s, pt: (row0 // n_q + b, col0 // width))

    def page_spec(u):
        return pl.BlockSpec((None, None, page * n_heads, d),
                            lambda b, s, pt: (layer, pt[b, n_pages - 1 - (s * pps + u)], 0, 0))

    grid_spec = pltpu.PrefetchScalarGridSpec(
        num_scalar_prefetch=1,
        grid=(n_seq, n_steps),
        in_specs=[new_spec(cols[0]), new_spec(cols[1]), new_spec(cols[2]),
                  pl.BlockSpec((rows, 1), lambda b, s, pt: (0, 0)),
                  pl.BlockSpec((page, page), lambda b, s, pt: (0, 0))]
                 + [page_spec(u) for u in range(pps)] * 2,
        out_specs=pl.BlockSpec((n_q, width), lambda b, s, pt: (b, 0)),
        scratch_shapes=[pltpu.VMEM((n_heads, 2 * n_q, d), BF16), pltpu.VMEM((rows, 1), F32),
                        pltpu.VMEM((rows, d), F32)],
    )
    return pl.pallas_call(
        functools.partial(_sb_sample_body, scale=d ** -0.5, n_heads=n_heads, n_q=n_q, d=d,
                          pages_per_step=pps),
        grid_spec=grid_spec,
        out_shape=jax.ShapeDtypeStruct((n_seq * n_q, width), F32),
        compiler_params=_cparams(("arbitrary", "arbitrary")),
        name="sb_sample",
    )(page_table, proj, proj, proj, bias_col, tri, *([ck] * pps), *([cv] * pps))


def _mlp_body(tok_ref, texp_ref, tact_ref, h_hbm, wg_ref, wu_ref, wd_ref, y_hbm,
              x_ref, land_ref, acc_ref, sem_in, sem_out, *tail_refs, tm, ch, tail_row0):
    del texp_ref
    i = pl.program_id(0)
    f = pl.program_id(1)
    nf = pl.num_programs(1)
    active = tact_ref[i] > 0
    d = acc_ref.shape[1]
    tn = _div_tile(d, 512, 128)

    def row_copy(tok, slot, r):
        return pltpu.make_async_copy(h_hbm.at[pl.ds(tok, 1)], land_ref.at[slot, pl.ds(r, 1)],
                                     sem_in.at[slot])

    def issue_chunk(c):
        def issue(r, carry):
            row_copy(tok_ref[i * tm + c * ch + r], c % 2, r).start()
            return carry

        lax.fori_loop(0, ch, issue, 0)

    def wait_chunk(c):
        def wait(r, carry):
            row_copy(0, c % 2, r).wait()
            return carry

        lax.fori_loop(0, ch, wait, 0)

    @pl.when(jnp.logical_and(active, f == 0))
    def _():
        n_chunks = tm // ch
        issue_chunk(0)
        acc_ref[...] = jnp.zeros_like(acc_ref)
        for c in range(n_chunks):
            if c + 1 < n_chunks:
                issue_chunk(c + 1)
            wait_chunk(c)
            x_ref[c * ch:(c + 1) * ch, :] = land_ref[c % 2].astype(BF16)

    @pl.when(active)
    def _():
        x = x_ref[...]
        g = _dot(x, wg_ref[...].astype(BF16))
        u = _dot(x, wu_ref[...].astype(BF16))
        a = (g * jax.nn.sigmoid(g) * u).astype(BF16)
        for n in range(d // tn):
            cols = slice(n * tn, (n + 1) * tn)
            acc_ref[:, cols] += _dot(a, wd_ref[:, cols].astype(BF16))

    @pl.when(jnp.logical_and(jnp.logical_not(active), f == 0))
    def _():
        acc_ref[...] = jnp.zeros_like(acc_ref)

    if tail_row0 is not None:
        xt_ref, acct_ref, sem_t = tail_refs
        tr = xt_ref.shape[0]
        tail_tile, tail_off = divmod(tail_row0, tm)
        in_tail_tile = i == tail_tile

        @pl.when(jnp.logical_and(in_tail_tile, f == 0))
        def _():
            load = pltpu.make_async_copy(h_hbm.at[pl.ds(tail_row0, tr)], xt_ref, sem_t)
            load.start()
            load.wait()
            acct_ref[...] = jnp.zeros_like(acct_ref)

        @pl.when(in_tail_tile)
        def _():
            xt = xt_ref[...]
            g = _dot_hp(xt, wg_ref[...])
            u = _dot_hp(xt, wu_ref[...])
            a = g * jax.nn.sigmoid(g) * u
            for n in range(d // tn):
                cols = slice(n * tn, (n + 1) * tn)
                acct_ref[:, cols] += _dot_hp(a, wd_ref[:, cols])

        @pl.when(jnp.logical_and(in_tail_tile, f == nf - 1))
        def _():
            acc_ref[tail_off:tail_off + tr, :] = acct_ref[...]

    @pl.when(f == nf - 1)
    def _():
        out = pltpu.make_async_copy(acc_ref, y_hbm.at[pl.ds(i * tm, tm)], sem_out)
        out.start()
        out.wait()


def fused_mlp(h, w_gate, w_up, w_down, tok_of_row, tile_expert, tile_active, *, tm, tf, tail_row0=None):
    d = h.shape[1]
    ff = w_gate.shape[-1]
    n_tiles = tile_expert.shape[0]
    nf = ff // tf
    ch = _div_tile(tm, 128, 16)
    tail_scratch = []
    if tail_row0 is not None:
        assert tail_row0 % tm + ROW_BLOCK <= tm
        tail_scratch = [pltpu.VMEM((ROW_BLOCK, d), F32), pltpu.VMEM((ROW_BLOCK, d), F32),
                        pltpu.SemaphoreType.DMA]

    def f_idx(i, f, tact):
        return jnp.where(tact[i] > 0, f, nf - 1)

    grid_spec = pltpu.PrefetchScalarGridSpec(
        num_scalar_prefetch=3,
        grid=(n_tiles, nf),
        in_specs=[pl.BlockSpec(memory_space=pl.ANY),
                  pl.BlockSpec((None, d, tf), lambda i, f, tok, texp, tact: (texp[i], 0, f_idx(i, f, tact))),
                  pl.BlockSpec((None, d, tf), lambda i, f, tok, texp, tact: (texp[i], 0, f_idx(i, f, tact))),
                  pl.BlockSpec((None, tf, d), lambda i, f, tok, texp, tact: (texp[i], f_idx(i, f, tact), 0))],
        out_specs=pl.BlockSpec(memory_space=pl.ANY),
        scratch_shapes=[pltpu.VMEM((tm, d), BF16), pltpu.VMEM((2, ch, d), F32), pltpu.VMEM((tm, d), F32),
                        pltpu.SemaphoreType.DMA((2,)), pltpu.SemaphoreType.DMA] + tail_scratch,
    )
    return pl.pallas_call(
        functools.partial(_mlp_body, tm=tm, ch=ch, tail_row0=tail_row0),
        grid_spec=grid_spec,
        out_shape=jax.ShapeDtypeStruct((n_tiles * tm, d), F32),
        compiler_params=_cparams(("arbitrary", "arbitrary"), V7X_VMEM_LIMIT_LARGE),
        name="fused_mlp",
    )(tok_of_row, tile_expert, tile_active, h, w_gate, w_up, w_down)


def _router_body(h_ref, w_ref, b_ref, o_ref, *, n_experts):
    logits = _dot_hp(h_ref[...], w_ref[...]) + b_ref[...]
    lane = lax.broadcasted_iota(jnp.int32, logits.shape, 1).astype(F32)
    neg = jnp.float32(-jnp.inf)
    none = jnp.float32(logits.shape[1])
    logits = jnp.where(lane < n_experts, logits, neg)
    v1 = jnp.max(logits, axis=-1, keepdims=True)
    i1 = jnp.min(jnp.where(logits == v1, lane, none), axis=-1, keepdims=True)
    rest = jnp.where(lane == i1, neg, logits)
    v2 = jnp.max(rest, axis=-1, keepdims=True)
    i2 = jnp.min(jnp.where(rest == v2, lane, none), axis=-1, keepdims=True)
    e2 = jnp.exp(v2 - v1)
    denom = 1.0 + e2
    out = jnp.where(lane == 0, i1,
                    jnp.where(lane == 1, i2,
                              jnp.where(lane == 2, 1.0 / denom,
                                        jnp.where(lane == 3, e2 / denom, 0.0))))
    o_ref[...] = out


def router(h, w_router, b_router, layer):
    rows, d = h.shape
    n_experts = w_router.shape[-1]
    lanes = 128
    w_pad = jnp.pad(w_router, ((0, 0), (0, 0), (0, lanes - n_experts)))
    b_pad = jnp.pad(b_router, ((0, 0), (0, lanes - n_experts))).reshape(-1, 1, lanes)
    tr = ROW_BLOCK
    return pl.pallas_call(
        functools.partial(_router_body, n_experts=n_experts),
        grid=(rows // tr,),
        in_specs=[pl.BlockSpec((tr, d), lambda i: (i, 0)),
                  pl.BlockSpec((None, d, lanes), lambda i: (layer, 0, 0)),
                  pl.BlockSpec((None, 1, lanes), lambda i: (layer, 0, 0))],
        out_specs=pl.BlockSpec((tr, lanes), lambda i: (i, 0)),
        out_shape=jax.ShapeDtypeStruct((rows, lanes), F32),
        compiler_params=_cparams(("arbitrary",)),
        name="router",
    )(h, w_pad, b_pad)


def _combine_body(p0_ref, p1_ref, y_hbm, w0_ref, w1_ref, o_ref, buf0_ref, buf1_ref, sem):
    i = pl.program_id(0)
    tt = o_ref.shape[0]

    def row_copy(src_row, buf_ref, r):
        return pltpu.make_async_copy(y_hbm.at[pl.ds(src_row, 1)], buf_ref.at[pl.ds(r, 1)], sem)

    def issue(r, carry):
        row_copy(p0_ref[i * tt + r], buf0_ref, r).start()
        row_copy(p1_ref[i * tt + r], buf1_ref, r).start()
        return carry

    def wait(r, carry):
        row_copy(0, buf0_ref, r).wait()
        row_copy(0, buf1_ref, r).wait()
        return carry

    lax.fori_loop(0, tt, issue, 0)
    lax.fori_loop(0, tt, wait, 0)
    o_ref[...] = w0_ref[...] * buf0_ref[...] + w1_ref[...] * buf1_ref[...]


def combine(y, p0, p1, w0, w1):
    rows = p0.shape[0]
    d = y.shape[1]
    tt = ROW_BLOCK
    grid_spec = pltpu.PrefetchScalarGridSpec(
        num_scalar_prefetch=2,
        grid=(rows // tt,),
        in_specs=[pl.BlockSpec(memory_space=pl.ANY),
                  pl.BlockSpec((tt, 1), lambda i, p0, p1: (i, 0)),
                  pl.BlockSpec((tt, 1), lambda i, p0, p1: (i, 0))],
        out_specs=pl.BlockSpec((tt, d), lambda i, p0, p1: (i, 0)),
        scratch_shapes=[pltpu.VMEM((tt, d), F32), pltpu.VMEM((tt, d), F32), pltpu.SemaphoreType.DMA],
    )
    return pl.pallas_call(
        _combine_body,
        grid_spec=grid_spec,
        out_shape=jax.ShapeDtypeStruct((rows, d), F32),
        compiler_params=_cparams(("arbitrary",)),
        name="combine",
    )(p0, p1, y, w0[:, None], w1[:, None])


def _moe_plan(route, n_tokens, n_rows, tm, n_tiles):
    e = route[:n_tokens, :TOP_K].astype(jnp.int32)
    w = route[:n_tokens, TOP_K:2 * TOP_K]
    onehot = (e[:, :, None] == jnp.arange(N_EXPERTS)[None, None, :]).astype(jnp.int32)
    per_tok = onehot.sum(axis=1)
    before = jnp.cumsum(per_tok, axis=0) - per_tok
    counts = per_tok.sum(axis=0)
    tiles_per = (counts + tm - 1) // tm
    tile_end = jnp.cumsum(tiles_per)
    tile_start = tile_end - tiles_per
    rank = jnp.take_along_axis(before, e, axis=1)
    pos = tile_start[e] * tm + rank
    tok = jnp.arange(n_tokens, dtype=jnp.int32)
    tok_of_row = jnp.zeros((n_tiles * tm,), jnp.int32)
    tok_of_row = tok_of_row.at[pos[:, 0]].set(tok).at[pos[:, 1]].set(tok)
    tile_ids = jnp.arange(n_tiles, dtype=jnp.int32)
    total = tile_end[-1]
    tile_expert = jnp.sum(jnp.minimum(tile_ids, total - 1)[:, None] >= tile_end[None, :], axis=1)
    tile_expert = jnp.minimum(tile_expert, N_EXPERTS - 1).astype(jnp.int32)
    tile_active = (tile_ids < total).astype(jnp.int32)
    pad = n_rows - n_tokens
    pos = jnp.pad(pos, ((0, pad), (0, 0))).astype(jnp.int32)
    w = jnp.pad(w, ((0, pad), (0, 0)))
    return tok_of_row, tile_expert, tile_active, pos, w


def kernel(x_prompt, x_sample, cache_k, cache_v, state_ret, page_table, c_prompt, c_sample, w_ada, b_ada, norm_mix, norm_ffn, w_in, b_sb, w_ret_br, w_sb_br, w_o, w_dense_gate, w_dense_up, w_dense_down, w_router, b_router, w_moe_gate, w_moe_up, w_moe_down, norm_final):
    n_seq, seq_len, d_model = x_prompt.shape
    dec_b, dec_s, _ = x_sample.shape
    depth = w_in.shape[0]
    ret_heads, dk, dv = state_ret.shape[2], state_ret.shape[3], state_ret.shape[4]
    sb_heads, sb_d = cache_k.shape[3], cache_k.shape[4]
    past_len = page_table.shape[1] * cache_k.shape[2]
    d_rqk, d_rv, d_sb = ret_heads * dk, ret_heads * dv, sb_heads * sb_d
    ff = w_dense_gate.shape[-1]

    n_prompt = n_seq * seq_len
    n_sample = dec_b * dec_s
    n_tok = n_prompt + n_sample
    assert n_sample <= ROW_BLOCK and seq_len % ROW_BLOCK == 0
    n_rows = n_prompt + ROW_BLOCK

    col = {}
    acc = 0
    for name, size in (("rq", d_rqk), ("rk", d_rqk), ("rv", d_rv), ("rg", d_rv),
                       ("sq", d_sb), ("sk", d_sb), ("sv", d_sb), ("ar", d_model), ("as", d_model)):
        col[name] = acc
        acc += size

    mm_tm = _div_tile(n_rows, 768, 16)
    in_tn = _div_tile(w_in.shape[-1], 512, 128)
    out_tn = _div_tile(d_model, 512, 128)
    merge_tn = _div_tile(math.gcd(d_model, col["ar"], col["as"]), 512, 128)
    mlp_tf = _div_tile(ff, 256, 128)
    dense_tm = _div_tile(n_rows, 960, 16)
    moe_tm = -(-(TOP_K * n_tok * 21) // (N_EXPERTS * 20 * 2) // 64) * 64

    x = jnp.concatenate([x_prompt.reshape(n_prompt, d_model), x_sample.reshape(n_sample, d_model),
                         jnp.zeros((n_rows - n_tok, d_model), F32)], axis=0)
    c_all = jnp.concatenate([c_prompt, c_sample,
                             jnp.zeros((-(n_seq + dec_b) % 8, d_model), F32)], axis=0)

    def mods_of(layer):
        mod = adaln(c_all, w_ada, b_ada, layer)
        mod_seq = mod.reshape(mod.shape[0], 6, 1, d_model)
        rows_mod = jnp.repeat(mod[n_seq:n_seq + dec_b].reshape(dec_b, 6, d_model), dec_s, axis=0)
        rows_mod = jnp.pad(rows_mod, ((0, ROW_BLOCK - n_sample), (0, 0), (0, 0)))
        return mod_seq, rows_mod.transpose(1, 0, 2)

    norm_kw = dict(seq_len=seq_len, n_seq=n_seq)
    SH_M, SC_M, GT_M, SH_F, SC_F, GT_F = range(6)

    outs = {k: [] for k in ("kp", "vp", "sp", "ks", "vs", "ss")}
    pending = None
    for layer in range(depth):
        mods = mods_of(layer)
        if pending is None:
            h, h_tail = norm_mod(x, norm_mix, layer, mods, tail_f32=True, **norm_kw)
        else:
            m_prev, gt_prev, mods_prev = pending
            x, h, h_tail = _resid_then_norm(x, m_prev, gt_prev, mods_prev, norm_mix, layer, mods,
                                            SC_M, SH_M, norm_kw)
        proj = matmul(h, h_tail, w_in, layer, tm=mm_tm, tn=in_tn)

        r_p, s_p = retention_prompt(proj, n_seq=n_seq, seq_len=seq_len, n_heads=ret_heads, dk=dk, dv=dv,
                                    cols=(col["rq"], col["rk"], col["rv"], col["rg"]))
        r_s, s_s = retention_sample(proj, state_ret, layer, row0=n_prompt, n_seq=dec_b, chunk=dec_s,
                                    past_len=past_len, n_heads=ret_heads, dk=dk, dv=dv,
                                    cols=(col["rq"], col["rk"], col["rv"], col["rg"]))
        bias = b_sb[layer].astype(F32)
        a_p = sb_prompt(proj, bias, n_seq=n_seq, seq_len=seq_len, n_heads=sb_heads, d=sb_d,
                        cols=(col["sq"], col["sk"], col["sv"]))
        a_s = sb_sample(proj, cache_k, cache_v, page_table, bias, layer, row0=n_prompt, n_seq=dec_b,
                        n_q=dec_s, n_heads=sb_heads, d=sb_d, cols=(col["sq"], col["sk"], col["sv"]))
        pad_rows = n_rows - n_tok
        r = jnp.concatenate([r_p, r_s.astype(BF16), jnp.zeros((pad_rows, d_rv), BF16)], axis=0)
        s = jnp.concatenate([a_p, a_s.astype(BF16), jnp.zeros((pad_rows, d_sb), BF16)], axis=0)
        r_tail = jnp.pad(r_s, ((0, pad_rows), (0, 0)))
        s_tail = jnp.pad(a_s, ((0, pad_rows), (0, 0)))
        u, u_tail = branch_merge(r, s, r_tail, s_tail, w_ret_br, w_sb_br, proj, layer,
                                 ar_col=col["ar"], as_col=col["as"], tm=mm_tm, tn=merge_tn)
        m = matmul(u, u_tail, w_o, layer, tm=mm_tm, tn=out_tn)

        kv_p = lambda c0: proj[:n_prompt, c0:c0 + d_sb].reshape(n_seq, seq_len, sb_heads, sb_d)
        kv_s = lambda c0: proj[n_prompt:n_tok, c0:c0 + d_sb].reshape(dec_b, dec_s, sb_heads, sb_d)
        outs["kp"].append(kv_p(col["sk"]))
        outs["vp"].append(kv_p(col["sv"]))
        outs["ks"].append(kv_s(col["sk"]))
        outs["vs"].append(kv_s(col["sv"]))
        outs["sp"].append(s_p)
        outs["ss"].append(s_s)

        idx = layer // 2
        x, h = norm_mod(x, norm_ffn, layer, mods, resid=(m, GT_M, SC_F, SH_F), out_dtype=F32, **norm_kw)
        if layer % 2 == 0:
            n_tiles = n_rows // dense_tm
            y = fused_mlp(h, w_dense_gate, w_dense_up, w_dense_down,
                          jnp.arange(n_rows, dtype=jnp.int32), jnp.full((n_tiles,), idx, jnp.int32),
                          jnp.ones((n_tiles,), jnp.int32), tm=dense_tm, tf=mlp_tf, tail_row0=n_prompt)
        else:
            route = router(h, w_router, b_router, idx)
            n_tiles = (TOP_K * n_tok) // moe_tm + N_EXPERTS
            tok_of_row, tile_expert, tile_active, pos, wts = _moe_plan(route, n_tok, n_rows, moe_tm, n_tiles)
            n_exp = w_moe_gate.shape[1]
            yg = fused_mlp(h, w_moe_gate.reshape(-1, d_model, ff), w_moe_up.reshape(-1, d_model, ff),
                           w_moe_down.reshape(-1, ff, d_model), tok_of_row, tile_expert + idx * n_exp,
                           tile_active, tm=moe_tm, tf=mlp_tf)
            y = combine(yg, pos[:, 0], pos[:, 1], wts[:, 0], wts[:, 1])
        pending = (y, GT_F, mods)

    m_prev, gt_prev, mods_prev = pending
    y_all = norm_mod(x, norm_final, 0, mods_prev, resid=(m_prev, gt_prev, 0, 0), modulate=False,
                     write_x=False, out_dtype=F32, **norm_kw)
    y_prompt = y_all[:n_prompt].reshape(n_seq, seq_len, d_model)
    y_sample = y_all[n_prompt:n_tok].reshape(dec_b, dec_s, d_model)
    st = lambda k: jnp.stack(outs[k])
    return (y_prompt, y_sample, st("kp"), st("vp"), st("sp"), st("ks"), st("vs"), st("ss"))


def _resid_then_norm(x, m_prev, gt_prev, mods_prev, g, layer, mods, sc_idx, sh_idx, norm_kw):
    mod_seq = jnp.concatenate([mods_prev[0][:, gt_prev:gt_prev + 1], mods[0]], axis=1)
    mod_rows = jnp.concatenate([mods_prev[1][gt_prev:gt_prev + 1], mods[1]], axis=0)
    return norm_mod(x, g, layer, (mod_seq, mod_rows), resid=(m_prev, 0, sc_idx + 1, sh_idx + 1),
                    tail_f32=True, **norm_kw)
```
